```python
import math
import jax
import jax.numpy as jnp
from jax import lax
import numpy as np

D_MODEL = 2048
BATCH = 2
SEQ = 4096
DEPTH = 4
DEC_BATCH = 8
DEC_SEQ = 4
PAST_LEN = 16384
PAGE_SIZE = 128

N_HEADS = 12
HEAD_DIM = 128
ATT_WIDTH = N_HEADS * HEAD_DIM
MOBA_BLOCK = 256
MOBA_TOPK = 3
Q_CHUNK = 32
ATT_SCALE = HEAD_DIM ** -0.5
N_BUCKETS = 32
MAX_DISTANCE = 128
POOL_WINDOWS = (2, 4, 8, 16)
POOL_GROUPS = len(POOL_WINDOWS)
POOL_GROUP_DIM = 128
POOL_WIDTH = POOL_GROUPS * POOL_GROUP_DIM
POOL_STATE = max(POOL_WINDOWS) - 1
IN_WIDTH = POOL_WIDTH + 3 * ATT_WIDTH + 2 * D_MODEL
PEER_HEADS = 8
PEER_QDIM = 256
PEER_NKEYS = 128
PEER_TOPK = 16
N_EXPERTS = PEER_NKEYS * PEER_NKEYS
PEER_CHUNK = 128
EPS = 1e-6

kernel_name = 'hybrid_pool_moba_peer_decoder_step'


def rms_norm(x, w):
    xf = x.astype(jnp.float32)
    y = xf * lax.rsqrt(jnp.mean(xf * xf, axis=-1, keepdims=True) + EPS)
    return (y * w.astype(jnp.float32)).astype(x.dtype)


def adaln(c, w_mod, b_mod):
    m = jnp.dot(jax.nn.silu(c), w_mod) + b_mod
    return jnp.split(m[:, None, :], 6, axis=-1)


def rel_bucket(dist):
    n = jnp.maximum(dist, 0)
    max_exact = N_BUCKETS // 2
    nf = jnp.maximum(n, 1).astype(jnp.float32)
    large = max_exact + (jnp.log(nf / max_exact) / math.log(MAX_DISTANCE / max_exact)
                         * (N_BUCKETS - max_exact)).astype(jnp.int32)
    return jnp.where(n < max_exact, n, jnp.minimum(large, N_BUCKETS - 1))


def mixer_inputs(x, shift, scale, norm_w, w_in, q_norm_w, k_norm_w):
    h = rms_norm(x, norm_w) * (1 + scale) + shift
    z = jnp.dot(h, w_in)
    b, t = x.shape[:2]
    o_q = POOL_WIDTH
    o_k = o_q + ATT_WIDTH
    o_v = o_k + ATT_WIDTH
    o_ga = o_v + ATT_WIDTH
    o_gb = o_ga + D_MODEL
    u = z[..., :o_q]
    q = rms_norm(z[..., o_q:o_k].reshape(b, t, N_HEADS, HEAD_DIM), q_norm_w)
    k = rms_norm(z[..., o_k:o_v].reshape(b, t, N_HEADS, HEAD_DIM), k_norm_w)
    v = z[..., o_v:o_ga].reshape(b, t, N_HEADS, HEAD_DIM)
    return u, q, k, v, z[..., o_ga:o_gb], z[..., o_gb:]


def pool_mix(u_ext, start_pos, w_grp, scale):
    T = u_ext.shape[1] - POOL_STATE
    uf = u_ext.astype(jnp.float32)
    cs = jnp.concatenate([jnp.zeros_like(uf[:, :1]), jnp.cumsum(uf, axis=1)], axis=1)
    pos = start_pos + jnp.arange(T)
    end = cs[:, POOL_STATE + 1:]
    outs = []
    for g, w in enumerate(POOL_WINDOWS):
        sl = slice(g * POOL_GROUP_DIM, (g + 1) * POOL_GROUP_DIM)
        begin = cs[:, POOL_STATE + 1 - w:POOL_STATE + 1 - w + T, sl]
        cnt = jnp.minimum(pos + 1, w).astype(jnp.float32)[None, :, None]
        d = (end[..., sl] - begin) / cnt - uf[:, POOL_STATE:, sl]
        outs.append(jnp.einsum('btc,cd->btd', d, w_grp[g].astype(jnp.float32)))
    y = jnp.concatenate(outs, axis=-1) * scale.astype(jnp.float32)
    return y.astype(u_ext.dtype)


def attend(s_sel, v_sel, s_own, v_own):
    if s_sel is None:
        p = jax.nn.softmax(s_own, axis=-1)
        return jnp.einsum('bhqp,bhpd->bhqd', p.astype(v_own.dtype), v_own,
                          preferred_element_type=jnp.float32)
    b, h, nq, nk, blk = s_sel.shape
    p = jax.nn.softmax(jnp.concatenate([s_sel.reshape(b, h, nq, nk * blk), s_own], axis=-1), axis=-1)
    p_sel = p[..., :nk * blk].reshape(s_sel.shape).astype(v_sel.dtype)
    p_own = p[..., nk * blk:].astype(v_own.dtype)
    return (jnp.einsum('bhqkp,bhqkpd->bhqd', p_sel, v_sel, preferred_element_type=jnp.float32)
            + jnp.einsum('bhqp,bhpd->bhqd', p_own, v_own, preferred_element_type=jnp.float32))


def moba_prompt(q, k, v, rel_bias):
    b, s = q.shape[:2]
    nb = -(-s // MOBA_BLOCK)
    pad = nb * MOBA_BLOCK - s

    def blocks(a):
        a = jnp.pad(a, ((0, 0), (0, pad), (0, 0), (0, 0)))
        return a.reshape(b, nb, MOBA_BLOCK, N_HEADS, HEAD_DIM).transpose(0, 3, 1, 2, 4)

    kb, vb = blocks(k), blocks(v)
    qh = q.transpose(0, 2, 1, 3)
    k_mean = jnp.mean(kb.astype(jnp.float32), axis=3)
    gate = jnp.einsum('bhsd,bhnd->bhsn', qh.astype(jnp.float32), k_mean)
    q_blk = jnp.arange(s) // MOBA_BLOCK
    gate = jnp.where(jnp.arange(nb)[None, None, None, :] < q_blk[None, None, :, None], gate, -jnp.inf)
    n_top = min(MOBA_TOPK, nb)
    _, idx = lax.top_k(gate, n_top)
    nc = s // Q_CHUNK

    def to_chunks(a):
        a = a.reshape(b, N_HEADS, nc, Q_CHUNK, *a.shape[3:])
        return jnp.moveaxis(a, 2, 0)

    b_ix = jnp.arange(b)[:, None, None, None]
    h_ix = jnp.arange(N_HEADS)[None, :, None, None]
    bias_hb = rel_bias.T

    def chunk(args):
        ci, qc, ic = args
        t = ci * Q_CHUNK + jnp.arange(Q_CHUNK)
        j = (ci * Q_CHUNK) // MOBA_BLOCK
        k_sel = kb[b_ix, h_ix, ic]
        v_sel = vb[b_ix, h_ix, ic]
        kpos = ic[..., None] * MOBA_BLOCK + jnp.arange(MOBA_BLOCK)
        s_sel = (jnp.einsum('bhqd,bhqkpd->bhqkp', qc, k_sel, preferred_element_type=jnp.float32) * ATT_SCALE
                 + bias_hb[h_ix[..., None], rel_bucket(t[:, None, None] - kpos)])
        valid = jnp.arange(n_top) < j
        s_sel = jnp.where(valid[:, None], s_sel, -jnp.inf)
        k_own = lax.dynamic_index_in_dim(kb, j, axis=2, keepdims=False)
        v_own = lax.dynamic_index_in_dim(vb, j, axis=2, keepdims=False)
        opos = j * MOBA_BLOCK + jnp.arange(MOBA_BLOCK)
        s_own = (jnp.einsum('bhqd,bhpd->bhqp', qc, k_own, preferred_element_type=jnp.float32) * ATT_SCALE
                 + rel_bias[rel_bucket(t[:, None] - opos[None, :])].transpose(2, 0, 1))
        s_own = jnp.where(opos[None, :] <= t[:, None], s_own, -jnp.inf)
        return attend(s_sel, v_sel, s_own, v_own)

    o = lax.map(chunk, (jnp.arange(nc), to_chunks(qh), to_chunks(idx)))
    return o.transpose(1, 0, 3, 2, 4).reshape(b, s, ATT_WIDTH).astype(q.dtype)


def moba_sample(q, k_new, v_new, cache_k, cache_v, page_table, layer, rel_bias):
    db, t_new = q.shape[:2]
    n_pages = page_table.shape[1]
    past_len = n_pages * PAGE_SIZE
    blk_pages = MOBA_BLOCK // PAGE_SIZE
    nb_past = past_len // MOBA_BLOCK
    own_first = nb_past * blk_pages
    own_pages = n_pages - own_first
    qh = q.transpose(0, 2, 1, 3)
    t = past_len + jnp.arange(t_new)
    h_ix = jnp.arange(N_HEADS)
    k_own, v_own = k_new, v_new
    if own_pages > 0:
        own_pt = page_table[:, own_first:]
        shp = (db, own_pages * PAGE_SIZE, N_HEADS, HEAD_DIM)
        k_own = jnp.concatenate([cache_k[own_pt, :, layer].reshape(shp).astype(k_new.dtype), k_new], axis=1)
        v_own = jnp.concatenate([cache_v[own_pt, :, layer].reshape(shp).astype(v_new.dtype), v_new], axis=1)
    k_own = k_own.transpose(0, 2, 1, 3)
    v_own = v_own.transpose(0, 2, 1, 3)
    opos = own_first * PAGE_SIZE + jnp.arange(k_own.shape[2])
    s_own = (jnp.einsum('bhtd,bhpd->bhtp', qh, k_own, preferred_element_type=jnp.float32) * ATT_SCALE
             + rel_bias[rel_bucket(t[:, None] - opos[None, :])].transpose(2, 0, 1))
    s_own = jnp.where(opos[None, :] <= t[:, None], s_own, -jnp.inf)
    if nb_past == 0:
        o = attend(None, None, s_own, v_own)
    else:
        k_past = cache_k[page_table[:, :own_first], :, layer]
        k_mean = jnp.mean(k_past.astype(jnp.float32).reshape(db, nb_past, MOBA_BLOCK, N_HEADS, HEAD_DIM),
                          axis=2).transpose(0, 2, 1, 3)
        gate = jnp.einsum('bhtd,bhnd->bhtn', qh.astype(jnp.float32), k_mean)
        n_top = min(MOBA_TOPK, nb_past)
        _, idx = lax.top_k(gate, n_top)
        b_ix = jnp.arange(db)[:, None, None, None, None]
        phys = page_table[b_ix, idx[..., None] * blk_pages + jnp.arange(blk_pages)]
        g_ix = (phys[..., None], jnp.arange(PAGE_SIZE), layer, h_ix[None, :, None, None, None, None])
        sel_shape = (db, N_HEADS, t_new, n_top, MOBA_BLOCK, HEAD_DIM)
        k_sel = cache_k[g_ix].reshape(sel_shape)
        v_sel = cache_v[g_ix].reshape(sel_shape)
        kpos = idx[..., None] * MOBA_BLOCK + jnp.arange(MOBA_BLOCK)
        s_sel = (jnp.einsum('bhtd,bhtkpd->bhtkp', qh, k_sel, preferred_element_type=jnp.float32) * ATT_SCALE
                 + rel_bias.T[h_ix[None, :, None, None, None], rel_bucket(t[None, None, :, None, None] - kpos)])
        o = attend(s_sel, v_sel, s_own, v_own)
    return o.transpose(0, 2, 1, 3).reshape(db, t_new, ATT_WIDTH).astype(q.dtype)


def mixer_merge(pool_y, att_o, g_a, g_b, w_pa, w_pb, w_out):
    m = jax.nn.sigmoid(g_a) * jnp.dot(pool_y, w_pa) + jax.nn.sigmoid(g_b) * jnp.dot(att_o, w_pb)
    return jnp.dot(m, w_out)


def peer(h, w_q, sub_keys, exp_u, exp_v):
    n = h.shape[0]
    q = jnp.dot(h, w_q).reshape(n, PEER_HEADS, 2, PEER_QDIM // 2)
    s = jnp.einsum('nhid,ikd->nhik', q, sub_keys, preferred_element_type=jnp.float32)
    s1, i1 = lax.top_k(s[:, :, 0], PEER_TOPK)
    s2, i2 = lax.top_k(s[:, :, 1], PEER_TOPK)
    cand = (s1[..., :, None] + s2[..., None, :]).reshape(n, PEER_HEADS, PEER_TOPK * PEER_TOPK)
    cand_id = (i1[..., :, None] * PEER_NKEYS + i2[..., None, :]).reshape(n, PEER_HEADS, PEER_TOPK * PEER_TOPK)
    best, pos = lax.top_k(cand, PEER_TOPK)
    e_id = jnp.take_along_axis(cand_id, pos, axis=-1)
    g = jax.nn.softmax(best, axis=-1)
    c = min(PEER_CHUNK, n)
    pad = (-n) % c
    hp = jnp.pad(h, ((0, pad), (0, 0))).reshape(-1, c, D_MODEL)
    ep = jnp.pad(e_id, ((0, pad), (0, 0), (0, 0))).reshape(-1, c, PEER_HEADS, PEER_TOPK)
    gp = jnp.pad(g, ((0, pad), (0, 0), (0, 0))).reshape(-1, c, PEER_HEADS, PEER_TOPK)

    def block(args):
        hb, eb, gb = args
        act = jax.nn.gelu(jnp.einsum('cd,chkd->chk', hb, exp_u[eb], preferred_element_type=jnp.float32),
                          approximate=False)
        w = (gb * act).astype(exp_v.dtype)
        return jnp.einsum('chk,chkd->cd', w, exp_v[eb], preferred_element_type=jnp.float32).astype(h.dtype)

    y = lax.map(block, (hp, ep, gp))
    return y.reshape(-1, D_MODEL)[:n]


def channel_mixer(x, shift, scale, gate, norm_w, w_q, sub_keys, exp_u, exp_v):
    h = rms_norm(x, norm_w) * (1 + scale) + shift
    y = peer(h.reshape(-1, D_MODEL), w_q, sub_keys, exp_u, exp_v).reshape(x.shape)
    return x + gate * y


def setup_inputs(seed: int = 0) -> dict:
    key = jax.random.key(seed)
    ks = jax.random.split(key, 32)
    f32 = jnp.float32
    n_pages = PAST_LEN // PAGE_SIZE
    n_used = DEC_BATCH * n_pages
    n_pool = n_used + max(1, n_used // 4)

    def nrm(k, shape, scale):
        return jax.random.normal(k, shape, f32) * scale

    def gain(k, shape):
        return 1.0 + 0.1 * jax.random.normal(k, shape, f32)

    page_table = jax.random.permutation(ks[7], n_pool)[:n_used].reshape(DEC_BATCH, n_pages).astype(jnp.int32)
    return {
        'x_prompt': nrm(ks[0], (BATCH, SEQ, D_MODEL), 1.0),
        'x_sample': nrm(ks[1], (DEC_BATCH, DEC_SEQ, D_MODEL), 1.0),
        'c_prompt': nrm(ks[2], (BATCH, D_MODEL), 1.0),
        'c_sample': nrm(ks[3], (DEC_BATCH, D_MODEL), 1.0),
        'cache_k': nrm(ks[4], (n_pool, PAGE_SIZE, DEPTH, N_HEADS, HEAD_DIM), 1.0),
        'cache_v': nrm(ks[5], (n_pool, PAGE_SIZE, DEPTH, N_HEADS, HEAD_DIM), 1.0),
        'state_pool': nrm(ks[6], (DEPTH, DEC_BATCH, POOL_STATE, POOL_WIDTH), 1.0),
        'page_table': page_table,
        'rel_bias': nrm(ks[8], (N_BUCKETS, N_HEADS), 0.5),
        'w_mod': nrm(ks[9], (DEPTH, D_MODEL, 6 * D_MODEL), 0.5 * D_MODEL ** -0.5),
        'b_mod': nrm(ks[10], (DEPTH, 6 * D_MODEL), 0.01),
        'norm1_w': gain(ks[11], (DEPTH, D_MODEL)),
        'w_in': nrm(ks[12], (DEPTH, D_MODEL, IN_WIDTH), D_MODEL ** -0.5),
        'q_norm_w': gain(ks[13], (DEPTH, HEAD_DIM)),
        'k_norm_w': gain(ks[14], (DEPTH, HEAD_DIM)),
        'pool_w': nrm(ks[15], (DEPTH, POOL_GROUPS, POOL_GROUP_DIM, POOL_GROUP_DIM), POOL_GROUP_DIM ** -0.5),
        'pool_scale': gain(ks[16], (DEPTH, POOL_WIDTH)),
        'w_pa': nrm(ks[17], (DEPTH, POOL_WIDTH, D_MODEL), POOL_WIDTH ** -0.5),
        'w_pb': nrm(ks[18], (DEPTH, ATT_WIDTH, D_MODEL), ATT_WIDTH ** -0.5),
        'w_out': nrm(ks[19], (DEPTH, D_MODEL, D_MODEL), D_MODEL ** -0.5),
        'norm2_w': gain(ks[20], (DEPTH, D_MODEL)),
        'peer_wq': nrm(ks[21], (DEPTH, D_MODEL, PEER_HEADS * PEER_QDIM), D_MODEL ** -0.5),
        'peer_sub_keys': nrm(ks[22], (DEPTH, 2, PEER_NKEYS, PEER_QDIM // 2), (PEER_QDIM // 2) ** -0.5),
        'peer_u': nrm(ks[23], (DEPTH, N_EXPERTS, D_MODEL), D_MODEL ** -0.5),
        'peer_v': nrm(ks[24], (DEPTH, N_EXPERTS, D_MODEL), PEER_HEADS ** -0.5),
    }


def reference(x_prompt, x_sample, c_prompt, c_sample, cache_k, cache_v, state_pool, page_table,
              rel_bias, w_mod, b_mod, norm1_w, w_in, q_norm_w, k_norm_w, pool_w, pool_scale,
              w_pa, w_pb, w_out, norm2_w, peer_wq, peer_sub_keys, peer_u, peer_v):
    past_len = page_table.shape[1] * PAGE_SIZE
    xp, xs = x_prompt, x_sample
    kp_rows, vp_rows, pool_p, ks_rows, vs_rows, pool_s = [], [], [], [], [], []
    for l in range(DEPTH):
        sh1p, sc1p, g1p, sh2p, sc2p, g2p = adaln(c_prompt, w_mod[l], b_mod[l])
        sh1s, sc1s, g1s, sh2s, sc2s, g2s = adaln(c_sample, w_mod[l], b_mod[l])
        u, q, k, v, ga, gb = mixer_inputs(xp, sh1p, sc1p, norm1_w[l], w_in[l], q_norm_w[l], k_norm_w[l])
        u_ext = jnp.concatenate([jnp.zeros((u.shape[0], POOL_STATE, POOL_WIDTH), u.dtype), u], axis=1)
        y_pool = pool_mix(u_ext, 0, pool_w[l], pool_scale[l])
        y_att = moba_prompt(q, k, v, rel_bias)
        xp = xp + g1p * mixer_merge(y_pool, y_att, ga, gb, w_pa[l], w_pb[l], w_out[l])
        xp = channel_mixer(xp, sh2p, sc2p, g2p, norm2_w[l], peer_wq[l], peer_sub_keys[l], peer_u[l], peer_v[l])
        kp_rows.append(k)
        vp_rows.append(v)
        pool_p.append(u_ext[:, -POOL_STATE:])
        u, q, k, v, ga, gb = mixer_inputs(xs, sh1s, sc1s, norm1_w[l], w_in[l], q_norm_w[l], k_norm_w[l])
        u_ext = jnp.concatenate([state_pool[l].astype(u.dtype), u], axis=1)
        y_pool = pool_mix(u_ext, past_len, pool_w[l], pool_scale[l])
        y_att = moba_sample(q, k, v, cache_k, cache_v, page_table, l, rel_bias)
        xs = xs + g1s * mixer_merge(y_pool, y_att, ga, gb, w_pa[l], w_pb[l], w_out[l])
        xs = channel_mixer(xs, sh2s, sc2s, g2s, norm2_w[l], peer_wq[l], peer_sub_keys[l], peer_u[l], peer_v[l])
        ks_rows.append(k)
        vs_rows.append(v)
        pool_s.append(u_ext[:, -POOL_STATE:])
    new_k_prompt = jnp.stack(kp_rows, axis=2)
    new_v_prompt = jnp.stack(vp_rows, axis=2)
    new_pool_prompt = jnp.stack(pool_p, axis=0)
    new_k_sample = jnp.stack(ks_rows, axis=2)
    new_v_sample = jnp.stack(vs_rows, axis=2)
    new_pool_sample = jnp.stack(pool_s, axis=0)
    return (xp, xs, new_k_prompt, new_v_prompt, new_pool_prompt, new_k_sample, new_v_sample, new_pool_sample)
```

```python
import functools
import math

import jax
import jax.numpy as jnp
from jax import lax
from jax.experimental import pallas as pl
from jax.experimental.pallas import tpu as pltpu

F32 = jnp.float32
BF16 = jnp.bfloat16
HIGHEST = lax.Precision.HIGHEST

N_HEADS = 12
HEAD_DIM = 128
ATT_WIDTH = N_HEADS * HEAD_DIM
MOBA_BLOCK = 256
MOBA_TOPK = 3
ATT_SCALE = HEAD_DIM ** -0.5
N_BUCKETS = 32
MAX_DISTANCE = 128
POOL_WINDOWS = (2, 4, 8, 16)
POOL_GROUP_DIM = 128
POOL_WIDTH = len(POOL_WINDOWS) * POOL_GROUP_DIM
POOL_STATE = max(POOL_WINDOWS) - 1
POOL_HALO = 16
PEER_HEADS = 8
PEER_QDIM = 256
PEER_NKEYS = 128
PEER_TOPK = 16
PEER_SLOTS = PEER_HEADS * PEER_TOPK
N_EXPERTS = PEER_NKEYS * PEER_NKEYS
PAGE_SIZE = 128
EPS = 1e-6
NEG_INF = float("-inf")

LANES = 128
SUBLANES = 8
VMEM_LIMIT = 56 * 1024 * 1024

TM = 256
TN_IN = 1536
TN_OUT = 1024
TP = 512
TT = 768
TE = 512
CH = 32

_NT = (((1,), (1,)), ((), ()))
_TN = (((0,), (0,)), ((), ()))


def _params(*sem):
    return pltpu.CompilerParams(dimension_semantics=sem, vmem_limit_bytes=VMEM_LIMIT)


def _mod_kernel(c_ref, w_ref, b_ref, o_ref):
    c = c_ref[...]
    s = (c * jax.nn.sigmoid(c)).astype(BF16)
    o_ref[...] = jnp.dot(s, w_ref[...].astype(BF16), preferred_element_type=F32) + b_ref[...]


def _modulation(c16, w_mod, b_mod):
    depth, d, n6 = w_mod.shape
    tn = 1536
    return pl.pallas_call(
        _mod_kernel,
        grid=(depth, n6 // tn),
        in_specs=[
            pl.BlockSpec((16, d), lambda l, j: (0, 0)),
            pl.BlockSpec((None, d, tn), lambda l, j: (l, 0, j)),
            pl.BlockSpec((None, 1, tn), lambda l, j: (l, 0, j)),
        ],
        out_specs=pl.BlockSpec((None, 16, tn), lambda l, j: (l, 0, j)),
        out_shape=jax.ShapeDtypeStruct((depth, 16, n6), F32),
        compiler_params=_params("parallel", "parallel"),
        name="adaln_mod",
    )(c16, w_mod, b_mod.reshape(depth, 1, n6))


def _tile_mod(modp_ref, mods_ref, comp, i, tiles_per_seq, n_ptiles, n_batch, tm):
    b = jnp.minimum(i // tiles_per_seq, n_batch - 1)
    row = modp_ref[comp, pl.ds(b, 1), :]
    srows = mods_ref[comp]
    samp = jnp.concatenate([srows, jnp.zeros((tm - srows.shape[0], srows.shape[1]), F32)], axis=0)
    return jnp.where(i >= n_ptiles, samp, row)


def _mixer_in_kernel(x_ref, modp_ref, mods_ref, nw_ref, w_ref, qn_ref, kn_ref, o_ref, *,
                     tiles_per_seq, n_ptiles, n_batch, jq, jk):
    j = pl.program_id(0)
    i = pl.program_id(1)
    x = x_ref[...]
    n = x * lax.rsqrt(jnp.mean(x * x, axis=-1, keepdims=True) + EPS) * nw_ref[...]
    sh = _tile_mod(modp_ref, mods_ref, 0, i, tiles_per_seq, n_ptiles, n_batch, TM)
    sc = _tile_mod(modp_ref, mods_ref, 1, i, tiles_per_seq, n_ptiles, n_batch, TM)
    h = (n * (1.0 + sc) + sh).astype(BF16)
    z = jnp.dot(h, w_ref[...], preferred_element_type=F32)

    def head_norm(wn_ref):
        for hh in range(N_HEADS):
            sl = slice(hh * HEAD_DIM, (hh + 1) * HEAD_DIM)
            zg = z[:, sl]
            o_ref[:, sl] = zg * lax.rsqrt(jnp.mean(zg * zg, axis=-1, keepdims=True) + EPS) * wn_ref[...]

    @pl.when(j == jq)
    def _():
        head_norm(qn_ref)

    @pl.when(j == jk)
    def _():
        head_norm(kn_ref)

    @pl.when(jnp.logical_and(j != jq, j != jk))
    def _():
        o_ref[...] = z


def _mixer_in(x, modp, mods, nw, w_bf, qn, kn, geom):
    n_pad, d = x.shape
    in_w = w_bf.shape[1]
    off_q = 2 * d + POOL_WIDTH
    assert off_q % TN_IN == 0 and ATT_WIDTH == TN_IN
    kern = functools.partial(_mixer_in_kernel, jq=off_q // TN_IN, jk=off_q // TN_IN + 1, **geom)
    return pl.pallas_call(
        kern,
        grid=(in_w // TN_IN, n_pad // TM),
        in_specs=[
            pl.BlockSpec((TM, d), lambda j, i: (i, 0)),
            pl.BlockSpec(modp.shape, lambda j, i: (0, 0, 0)),
            pl.BlockSpec(mods.shape, lambda j, i: (0, 0, 0)),
            pl.BlockSpec((1, d), lambda j, i: (0, 0)),
            pl.BlockSpec((d, TN_IN), lambda j, i: (0, j)),
            pl.BlockSpec((1, HEAD_DIM), lambda j, i: (0, 0)),
            pl.BlockSpec((1, HEAD_DIM), lambda j, i: (0, 0)),
        ],
        out_specs=pl.BlockSpec((TM, TN_IN), lambda j, i: (i, j)),
        out_shape=jax.ShapeDtypeStruct((n_pad, in_w), F32),
        compiler_params=_params("parallel", "parallel"),
        name="mixer_in",
    )(x, modp, mods, nw, w_bf, qn, kn)


def _pool_compute(ext, pos_first, n_rows, pw_ref, ps_ref):
    rowpos = pos_first + lax.broadcasted_iota(jnp.int32, (n_rows, 1), 0)
    outs = []
    for g, w in enumerate(POOL_WINDOWS):
        sl = slice(g * POOL_GROUP_DIM, (g + 1) * POOL_GROUP_DIM)
        xg = ext[:, sl]
        s = xg
        k = 1
        while k < w:
            s = s + pltpu.roll(s, k, axis=0)
            k *= 2
        cnt = jnp.minimum(rowpos + 1, w).astype(F32)
        dlt = s[POOL_HALO:] / cnt - xg[POOL_HALO:]
        y = jnp.dot(dlt.astype(BF16), pw_ref[g].astype(BF16), preferred_element_type=F32)
        outs.append(y * ps_ref[:, sl])
    return jnp.concatenate(outs, axis=-1)


def _pool_prompt_kernel(cur_ref, halo_ref, pw_ref, ps_ref, o_ref):
    t = pl.program_id(1)
    halo = jnp.where(t == 0, 0.0, halo_ref[...])
    ext = jnp.concatenate([halo, cur_ref[...]], axis=0)
    o_ref[...] = _pool_compute(ext, t * TP, TP, pw_ref, ps_ref)


def _pool_prompt(z, pool_w, pool_scale, n_batch, seq, off_u):
    tps = seq // TP
    ub = off_u // POOL_WIDTH
    return pl.pallas_call(
        _pool_prompt_kernel,
        grid=(n_batch, tps),
        in_specs=[
            pl.BlockSpec((TP, POOL_WIDTH), lambda b, t: (b * tps + t, ub)),
            pl.BlockSpec((POOL_HALO, POOL_WIDTH),
                         lambda b, t: (jnp.maximum((b * seq + t * TP) // POOL_HALO - 1, 0), ub)),
            pl.BlockSpec(pool_w.shape, lambda b, t: (0, 0, 0)),
            pl.BlockSpec((1, POOL_WIDTH), lambda b, t: (0, 0)),
        ],
        out_specs=pl.BlockSpec((TP, POOL_WIDTH), lambda b, t: (b * tps + t, 0)),
        out_shape=jax.ShapeDtypeStruct((n_batch * seq, POOL_WIDTH), F32),
        compiler_params=_params("parallel", "parallel"),
        name="pool_prompt",
    )(z, z, pool_w, pool_scale)


def _top_blocks(gate, n_valid, n_blocks):
    col = lax.broadcasted_iota(jnp.int32, gate.shape, 1)
    g = jnp.where(col < n_valid, gate, NEG_INF)
    sels = []
    for r in range(MOBA_TOPK):
        m = jnp.max(g, axis=1, keepdims=True)
        idx = jnp.min(jnp.where(g == m, col, n_blocks), axis=1, keepdims=True)
        sels.append(jnp.where(r < n_valid, idx, -1))
        g = jnp.where(col == idx, NEG_INF, g)
    return sels


def _moba_prompt_kernel(q_ref, k_ref, v_ref, bias_ref, far_ref, o_ref, kmean_ref, *, n_blocks):
    j = pl.program_id(2)
    blk = MOBA_BLOCK

    @pl.when(j == 0)
    def _():
        for n in range(n_blocks):
            kmean_ref[n:n + 1, :] = jnp.sum(k_ref[n * blk:(n + 1) * blk, :], axis=0,
                                            keepdims=True) * (1.0 / blk)

    q = q_ref[...]
    gate = lax.dot_general(q, kmean_ref[...], _NT, precision=HIGHEST, preferred_element_type=F32)
    sels = _top_blocks(gate, j, n_blocks)
    qb = q.astype(BF16)
    far = far_ref[0:1, 0:1]

    def scores(n):
        kb = k_ref[pl.ds(pl.multiple_of(n * blk, blk), blk), :].astype(BF16)
        return lax.dot_general(qb, kb, _NT, preferred_element_type=F32) * ATT_SCALE

    def values(n):
        return v_ref[pl.ds(pl.multiple_of(n * blk, blk), blk), :].astype(BF16)

    row = lax.broadcasted_iota(jnp.int32, (blk, blk), 0)
    col = lax.broadcasted_iota(jnp.int32, (blk, blk), 1)
    s = jnp.where(col <= row, scores(j) + bias_ref[0], NEG_INF)
    m = jnp.max(s, axis=1, keepdims=True)
    p = jnp.exp(s - m)
    l = jnp.sum(p, axis=1, keepdims=True)
    acc = jnp.dot(p.astype(BF16), values(j), preferred_element_type=F32)

    def body(n, carry):
        m, l, acc = carry
        bias = jnp.where(n == j - 1, bias_ref[1], far)
        picked = (sels[0] == n) | (sels[1] == n) | (sels[2] == n)
        s = jnp.where(picked, scores(n) + bias, NEG_INF)
        m_new = jnp.maximum(m, jnp.max(s, axis=1, keepdims=True))
        alpha = jnp.exp(m - m_new)
        p = jnp.exp(s - m_new)
        l = alpha * l + jnp.sum(p, axis=1, keepdims=True)
        acc = alpha * acc + jnp.dot(p.astype(BF16), values(n), preferred_element_type=F32)
        return m_new, l, acc

    m, l, acc = lax.fori_loop(0, j, body, (m, l, acc))
    o_ref[...] = acc / l


def _moba_prompt(z, bias_tiles, far, n_batch, seq, off_q):
    n_blocks = seq // MOBA_BLOCK
    qb0 = off_q // HEAD_DIM
    kb0 = qb0 + N_HEADS
    vb0 = kb0 + N_HEADS
    return pl.pallas_call(
        functools.partial(_moba_prompt_kernel, n_blocks=n_blocks),
        grid=(n_batch, N_HEADS, n_blocks),
        in_specs=[
            pl.BlockSpec((MOBA_BLOCK, HEAD_DIM), lambda b, h, j: (b * n_blocks + j, qb0 + h)),
            pl.BlockSpec((seq, HEAD_DIM), lambda b, h, j: (b, kb0 + h)),
            pl.BlockSpec((seq, HEAD_DIM), lambda b, h, j: (b, vb0 + h)),
            pl.BlockSpec((None, 2, MOBA_BLOCK, MOBA_BLOCK), lambda b, h, j: (h, 0, 0, 0)),
            pl.BlockSpec((None, SUBLANES, LANES), lambda b, h, j: (h, 0, 0)),
        ],
        out_specs=pl.BlockSpec((MOBA_BLOCK, HEAD_DIM), lambda b, h, j: (b * n_blocks + j, h)),
        out_shape=jax.ShapeDtypeStruct((n_batch * seq, ATT_WIDTH), F32),
        scratch_shapes=[pltpu.VMEM((n_blocks, HEAD_DIM), F32)],
        compiler_params=_params("parallel", "parallel", "arbitrary"),
        name="moba_prompt",
    )(z, z, z, bias_tiles, far)


def _kmean_kernel(pt_ref, c_ref, o_ref, *, pages_per_block):
    p = pl.program_id(1)
    s = jnp.sum(c_ref[...], axis=0)

    @pl.when(p % pages_per_block == 0)
    def _():
        o_ref[...] = s

    @pl.when(p % pages_per_block != 0)
    def _():
        o_ref[...] = o_ref[...] + s

    @pl.when(p % pages_per_block == pages_per_block - 1)
    def _():
        o_ref[...] = o_ref[...] * (1.0 / MOBA_BLOCK)


def _head_major(cache):
    n_pool, page, depth, heads, hd = cache.shape
    return cache.transpose(0, 1, 3, 2, 4).reshape(n_pool, page, heads * depth, hd)


def _cache_block_means(cache_k4, page_table):
    n_pool, page, rows, hd = cache_k4.shape
    db, n_pages = page_table.shape
    ppb = MOBA_BLOCK // page
    grid_spec = pltpu.PrefetchScalarGridSpec(
        num_scalar_prefetch=1,
        grid=(db, n_pages),
        in_specs=[pl.BlockSpec((None, page, rows, hd), lambda b, p, pt: (pt[b * n_pages + p], 0, 0, 0))],
        out_specs=pl.BlockSpec((None, None, rows, hd), lambda b, p, pt: (b, p // ppb, 0, 0)),
    )
    return pl.pallas_call(
        functools.partial(_kmean_kernel, pages_per_block=ppb),
        grid_spec=grid_spec,
        out_shape=jax.ShapeDtypeStruct((db, n_pages // ppb, rows, hd), F32),
        compiler_params=_params("parallel", "arbitrary"),
        name="cache_block_means",
    )(page_table.reshape(-1), cache_k4)


def _sample_pre_kernel(u_ref, st_ref, q_ref, km_ref, pw_ref, ps_ref, yp_ref, idx_ref, *,
                       past_len, n_blocks):
    ext = jnp.concatenate([st_ref[...], u_ref[...]], axis=0)
    yp_ref[...] = _pool_compute(ext, past_len, SUBLANES, pw_ref, ps_ref)
    lane = lax.broadcasted_iota(jnp.int32, (SUBLANES, LANES), 1)
    for h in range(N_HEADS):
        qh = q_ref[:, h * HEAD_DIM:(h + 1) * HEAD_DIM]
        gate = lax.dot_general(qh, km_ref[h], _NT, precision=HIGHEST, preferred_element_type=F32)
        sels = _top_blocks(gate, n_blocks, n_blocks)
        out = jnp.zeros((SUBLANES, LANES), jnp.int32)
        for r in range(MOBA_TOPK):
            out = jnp.where(lane == r, sels[r], out)
        idx_ref[h] = out


def _sample_pre(zs8, state16, kmean_l, pool_w, pool_scale, past_len, off_u, off_q):
    db = zs8.shape[0]
    n_blocks = kmean_l.shape[2]
    return pl.pallas_call(
        functools.partial(_sample_pre_kernel, past_len=past_len, n_blocks=n_blocks),
        grid=(db,),
        in_specs=[
            pl.BlockSpec((None, SUBLANES, POOL_WIDTH), lambda b: (b, 0, off_u // POOL_WIDTH)),
            pl.BlockSpec((None, POOL_HALO, POOL_WIDTH), lambda b: (b, 0, 0)),
            pl.BlockSpec((None, SUBLANES, ATT_WIDTH), lambda b: (b, 0, off_q // ATT_WIDTH)),
            pl.BlockSpec((None, N_HEADS, n_blocks, HEAD_DIM), lambda b: (b, 0, 0, 0)),
            pl.BlockSpec(pool_w.shape, lambda b: (0, 0, 0)),
            pl.BlockSpec((1, POOL_WIDTH), lambda b: (0, 0)),
        ],
        out_specs=[
            pl.BlockSpec((None, SUBLANES, POOL_WIDTH), lambda b: (b, 0, 0)),
            pl.BlockSpec((None, N_HEADS, SUBLANES, LANES), lambda b: (b, 0, 0, 0)),
        ],
        out_shape=[
            jax.ShapeDtypeStruct((db, SUBLANES, POOL_WIDTH), F32),
            jax.ShapeDtypeStruct((db, N_HEADS, SUBLANES, LANES), jnp.int32),
        ],
        compiler_params=_params("parallel"),
        name="sample_pre",
    )(zs8, state16, zs8, kmean_l, pool_w, pool_scale)


def _sample_att_kernel(pages_ref, sel_ref, q_ref, kn_ref, vn_ref, bias_ref, far_ref, ck_ref, cv_ref,
                       o_ref, kbuf, vbuf, sem, *, layer, depth, t_new, last_block, pages_per_block):
    b = pl.program_id(0)
    h = pl.program_id(1)
    base = (b * N_HEADS + h) * t_new * MOBA_TOPK
    lh = h * depth + layer

    def copies(t, r, pg):
        page = pages_ref[(base + t * MOBA_TOPK + r) * pages_per_block + pg]
        dst = pl.ds(pg * PAGE_SIZE, PAGE_SIZE)
        return (pltpu.make_async_copy(ck_ref.at[page, :, lh, :], kbuf.at[t, r, dst, :], sem.at[0]),
                pltpu.make_async_copy(cv_ref.at[page, :, lh, :], vbuf.at[t, r, dst, :], sem.at[1]))

    slots = [(t, r, pg) for t in range(t_new) for r in range(MOBA_TOPK) for pg in range(pages_per_block)]
    for s in slots:
        for c in copies(*s):
            c.start()
    for s in slots:
        for c in copies(*s):
            c.wait()

    q = q_ref[...]
    qb = q.astype(BF16)
    far = far_ref[0:1, 0:1]
    row = lax.broadcasted_iota(jnp.int32, (SUBLANES, SUBLANES), 0)
    col = lax.broadcasted_iota(jnp.int32, (SUBLANES, SUBLANES), 1)
    s_own = lax.dot_general(qb, kn_ref[...].astype(BF16), _NT, preferred_element_type=F32) * ATT_SCALE
    s_own = jnp.where(jnp.logical_and(col <= row, col < t_new), s_own + bias_ref[0, :, 0:SUBLANES], NEG_INF)
    vn = vn_ref[...].astype(BF16)
    orow = lax.broadcasted_iota(jnp.int32, (SUBLANES, HEAD_DIM), 0)
    out = jnp.zeros((SUBLANES, HEAD_DIM), F32)
    n_sel = MOBA_TOPK * MOBA_BLOCK
    for t in range(t_new):
        kt = kbuf[t].reshape(n_sel, HEAD_DIM).astype(BF16)
        vt = vbuf[t].reshape(n_sel, HEAD_DIM).astype(BF16)
        s_sel = lax.dot_general(qb, kt, _NT, preferred_element_type=F32) * ATT_SCALE
        bias = jnp.concatenate(
            [jnp.where(sel_ref[base + t * MOBA_TOPK + r] == last_block, bias_ref[1],
                       jnp.broadcast_to(far, (SUBLANES, MOBA_BLOCK))) for r in range(MOBA_TOPK)], axis=1)
        s_sel = s_sel + bias
        m = jnp.maximum(jnp.max(s_sel, axis=1, keepdims=True), jnp.max(s_own, axis=1, keepdims=True))
        p_sel = jnp.exp(s_sel - m)
        p_own = jnp.exp(s_own - m)
        l = jnp.sum(p_sel, axis=1, keepdims=True) + jnp.sum(p_own, axis=1, keepdims=True)
        o_t = (jnp.dot(p_sel.astype(BF16), vt, preferred_element_type=F32)
               + jnp.dot(p_own.astype(BF16), vn, preferred_element_type=F32)) / l
        out = jnp.where(orow == t, o_t, out)
    o_ref[...] = out


def _sample_att(pages, sel, zs8, bias_tiles, far, cache_k4, cache_v4, layer, depth, t_new, n_blocks, off_q):
    db = zs8.shape[0]
    page = cache_k4.shape[1]
    ppb = MOBA_BLOCK // page
    qb0 = off_q // HEAD_DIM
    kb0 = qb0 + N_HEADS
    vb0 = kb0 + N_HEADS
    grid_spec = pltpu.PrefetchScalarGridSpec(
        num_scalar_prefetch=2,
        grid=(db, N_HEADS),
        in_specs=[
            pl.BlockSpec((None, SUBLANES, HEAD_DIM), lambda b, h, pg, sl: (b, 0, qb0 + h)),
            pl.BlockSpec((None, SUBLANES, HEAD_DIM), lambda b, h, pg, sl: (b, 0, kb0 + h)),
            pl.BlockSpec((None, SUBLANES, HEAD_DIM), lambda b, h, pg, sl: (b, 0, vb0 + h)),
            pl.BlockSpec((None, 2, SUBLANES, MOBA_BLOCK), lambda b, h, pg, sl: (h, 0, 0, 0)),
            pl.BlockSpec((None, SUBLANES, LANES), lambda b, h, pg, sl: (h, 0, 0)),
            pl.BlockSpec(memory_space=pl.ANY),
            pl.BlockSpec(memory_space=pl.ANY),
        ],
        out_specs=pl.BlockSpec((None, SUBLANES, HEAD_DIM), lambda b, h, pg, sl: (b, 0, h)),
        scratch_shapes=[
            pltpu.VMEM((t_new, MOBA_TOPK, MOBA_BLOCK, HEAD_DIM), F32),
            pltpu.VMEM((t_new, MOBA_TOPK, MOBA_BLOCK, HEAD_DIM), F32),
            pltpu.SemaphoreType.DMA((2,)),
        ],
    )
    return pl.pallas_call(
        functools.partial(_sample_att_kernel, layer=layer, depth=depth, t_new=t_new,
                          last_block=n_blocks - 1, pages_per_block=ppb),
        grid_spec=grid_spec,
        out_shape=jax.ShapeDtypeStruct((db, SUBLANES, ATT_WIDTH), F32),
        compiler_params=_params("arbitrary", "arbitrary"),
        name="sample_att",
    )(pages, sel, zs8, zs8, zs8, bias_tiles, far, cache_k4, cache_v4)


def _merge_kernel(yp_ref, ya_ref, ga_ref, gb_ref, x_ref, modp_ref, mods_ref, wpa_ref, wpb_ref, wo_ref,
                  o_ref, m_sc, *, tiles_per_seq, n_ptiles, n_batch):
    i = pl.program_id(0)
    j = pl.program_id(1)

    @pl.when(j == 0)
    def _():
        a = jnp.dot(yp_ref[...].astype(BF16), wpa_ref[...], preferred_element_type=F32)
        c = jnp.dot(ya_ref[...].astype(BF16), wpb_ref[...], preferred_element_type=F32)
        m = jax.nn.sigmoid(ga_ref[...]) * a + jax.nn.sigmoid(gb_ref[...]) * c
        m_sc[...] = m.astype(BF16)

    g1 = _tile_mod(modp_ref, mods_ref, 0, i, tiles_per_seq, n_ptiles, n_batch, TM)
    r = jnp.dot(m_sc[...], wo_ref[...], preferred_element_type=F32)
    o_ref[...] = x_ref[...] + g1 * r


def _merge(ypool, yatt, z, x, modp, mods, wpa, wpb, wo, geom):
    n_pad, d = x.shape
    return pl.pallas_call(
        functools.partial(_merge_kernel, **geom),
        grid=(n_pad // TM, d // TN_OUT),
        in_specs=[
            pl.BlockSpec((TM, POOL_WIDTH), lambda i, j: (i, 0)),
            pl.BlockSpec((TM, ATT_WIDTH), lambda i, j: (i, 0)),
            pl.BlockSpec((TM, d), lambda i, j: (i, 0)),
            pl.BlockSpec((TM, d), lambda i, j: (i, 1)),
            pl.BlockSpec((TM, TN_OUT), lambda i, j: (i, j)),
            pl.BlockSpec((1, SUBLANES, TN_OUT), lambda i, j: (0, 0, j)),
            pl.BlockSpec((1, mods.shape[1], TN_OUT), lambda i, j: (0, 0, j)),
            pl.BlockSpec(wpa.shape, lambda i, j: (0, 0)),
            pl.BlockSpec(wpb.shape, lambda i, j: (0, 0)),
            pl.BlockSpec((d, TN_OUT), lambda i, j: (0, j)),
        ],
        out_specs=pl.BlockSpec((TM, TN_OUT), lambda i, j: (i, j)),
        out_shape=jax.ShapeDtypeStruct((n_pad, d), F32),
        scratch_shapes=[pltpu.VMEM((TM, d), BF16)],
        compiler_params=_params("parallel", "arbitrary"),
        name="mixer_merge",
    )(ypool, yatt, z, z, x, modp, mods, wpa, wpb, wo)


def _top16_rows(x):
    rows, t = x.shape
    rid = lax.broadcasted_iota(jnp.int32, (rows, t), 0)
    r16 = lax.broadcasted_iota(jnp.int32, (PEER_TOPK, t), 0)

    def body(k, carry):
        x, vals, idxs = carry
        m = jnp.max(x, axis=0, keepdims=True)
        i = jnp.min(jnp.where(x == m, rid, rows), axis=0, keepdims=True)
        vals = jnp.where(r16 == k, m, vals)
        idxs = jnp.where(r16 == k, i, idxs)
        x = jnp.where(rid == i, NEG_INF, x)
        return x, vals, idxs

    _, vals, idxs = lax.fori_loop(
        0, PEER_TOPK, body, (x, jnp.zeros((PEER_TOPK, t), F32), jnp.zeros((PEER_TOPK, t), jnp.int32)))
    return vals, idxs


def _route_kernel(x_ref, modp_ref, mods_ref, nw_ref, wq_ref, sk_ref, h_ref, eid_ref, g_ref, *,
                  tiles_per_seq, n_ptiles, n_batch):
    i = pl.program_id(0)
    x = x_ref[...]
    n = x * lax.rsqrt(jnp.mean(x * x, axis=-1, keepdims=True) + EPS) * nw_ref[...]
    sh = _tile_mod(modp_ref, mods_ref, 0, i, tiles_per_seq, n_ptiles, n_batch, TM)
    sc = _tile_mod(modp_ref, mods_ref, 1, i, tiles_per_seq, n_ptiles, n_batch, TM)
    hb = (n * (1.0 + sc) + sh).astype(BF16)
    h_ref[...] = hb
    hq = jnp.dot(hb, wq_ref[...], preferred_element_type=F32)
    half = PEER_QDIM // 2
    for hd in range(PEER_HEADS):
        q1 = hq[:, hd * PEER_QDIM:hd * PEER_QDIM + half]
        q2 = hq[:, hd * PEER_QDIM + half:(hd + 1) * PEER_QDIM]
        s1 = lax.dot_general(sk_ref[0], q1, _NT, precision=HIGHEST, preferred_element_type=F32)
        s2 = lax.dot_general(sk_ref[1], q2, _NT, precision=HIGHEST, preferred_element_type=F32)
        v1, i1 = _top16_rows(s1)
        v2, i2 = _top16_rows(s2)
        cand = jnp.concatenate([v1[a:a + 1, :] + v2 for a in range(PEER_TOPK)], axis=0)
        best, pos = _top16_rows(cand)
        pa = pos // PEER_TOPK
        pb = pos % PEER_TOPK
        e1 = jnp.zeros_like(pos)
        e2 = jnp.zeros_like(pos)
        for a in range(PEER_TOPK):
            e1 = jnp.where(pa == a, i1[a:a + 1, :], e1)
            e2 = jnp.where(pb == a, i2[a:a + 1, :], e2)
        mx = jnp.max(best, axis=0, keepdims=True)
        ex = jnp.exp(best - mx)
        g = ex / jnp.sum(ex, axis=0, keepdims=True)
        sl = slice(hd * PEER_TOPK, (hd + 1) * PEER_TOPK)
        eid_ref[sl, :] = e1 * PEER_NKEYS + e2
        g_ref[sl, :] = g


def _route(x, modp, mods, nw, wq, sub_keys, geom):
    n_pad, d = x.shape
    return pl.pallas_call(
        functools.partial(_route_kernel, **geom),
        grid=(n_pad // TM,),
        in_specs=[
            pl.BlockSpec((TM, d), lambda i: (i, 0)),
            pl.BlockSpec(modp.shape, lambda i: (0, 0, 0)),
            pl.BlockSpec(mods.shape, lambda i: (0, 0, 0)),
            pl.BlockSpec((1, d), lambda i: (0, 0)),
            pl.BlockSpec(wq.shape, lambda i: (0, 0)),
            pl.BlockSpec(sub_keys.shape, lambda i: (0, 0, 0)),
        ],
        out_specs=[
            pl.BlockSpec((TM, d), lambda i: (i, 0)),
            pl.BlockSpec((PEER_SLOTS, TM), lambda i: (0, i)),
            pl.BlockSpec((PEER_SLOTS, TM), lambda i: (0, i)),
        ],
        out_shape=[
            jax.ShapeDtypeStruct((n_pad, d), BF16),
            jax.ShapeDtypeStruct((PEER_SLOTS, n_pad), jnp.int32),
            jax.ShapeDtypeStruct((PEER_SLOTS, n_pad), F32),
        ],
        compiler_params=_params("parallel"),
        name="peer_route",
    )(x, modp, mods, nw, wq, sub_keys)


def _expert_kernel(h_ref, u_ref, v_ref, eid_ref, g_ref, o_ref, at_sc, pt_sc):
    j = pl.program_id(1)
    at_sc[...] = lax.dot_general(u_ref[...], h_ref[...], _NT, preferred_element_type=F32)
    e0 = j * TE

    def chunk(c, carry):
        base = pl.multiple_of(c * CH, CH)
        tgt = lax.broadcasted_iota(jnp.int32, (CH, TT), 0) + (e0 + base)
        w = jnp.zeros((CH, TT), F32)
        for s in range(PEER_SLOTS):
            w = w + jnp.where(eid_ref[s:s + 1, :] == tgt, g_ref[s:s + 1, :], 0.0)
        a = at_sc[pl.ds(base, CH), :]
        act = 0.5 * a * (1.0 + lax.erf(a * (1.0 / math.sqrt(2.0))))
        pt_sc[pl.ds(base, CH), :] = (w * act).astype(BF16)
        return carry

    lax.fori_loop(0, TE // CH, chunk, 0)
    y = lax.dot_general(pt_sc[...], v_ref[...], _TN, preferred_element_type=F32)

    @pl.when(j == 0)
    def _():
        o_ref[...] = y

    @pl.when(j != 0)
    def _():
        o_ref[...] = o_ref[...] + y


def _experts(h2, u_bf, v_bf, eid, g):
    n_pad, d = h2.shape
    n_exp = u_bf.shape[0]
    return pl.pallas_call(
        _expert_kernel,
        grid=(n_pad // TT, n_exp // TE),
        in_specs=[
            pl.BlockSpec((TT, d), lambda i, j: (i, 0)),
            pl.BlockSpec((TE, d), lambda i, j: (j, 0)),
            pl.BlockSpec((TE, d), lambda i, j: (j, 0)),
            pl.BlockSpec((PEER_SLOTS, TT), lambda i, j: (0, i)),
            pl.BlockSpec((PEER_SLOTS, TT), lambda i, j: (0, i)),
        ],
        out_specs=pl.BlockSpec((TT, d), lambda i, j: (i, 0)),
        out_shape=jax.ShapeDtypeStruct((n_pad, d), F32),
        scratch_shapes=[pltpu.VMEM((TE, TT), F32), pltpu.VMEM((TE, TT), BF16)],
        compiler_params=_params("parallel", "arbitrary"),
        name="peer_experts",
    )(h2, u_bf, v_bf, eid, g)


def _residual_kernel(x_ref, y_ref, modp_ref, mods_ref, o_ref, *, tiles_per_seq, n_ptiles, n_batch):
    i = pl.program_id(0)
    g2 = _tile_mod(modp_ref, mods_ref, 0, i, tiles_per_seq, n_ptiles, n_batch, TM)
    o_ref[...] = x_ref[...] + g2 * y_ref[...]


def _residual(x, y, modp, mods, geom):
    n_pad, d = x.shape
    return pl.pallas_call(
        functools.partial(_residual_kernel, **geom),
        grid=(n_pad // TM,),
        in_specs=[
            pl.BlockSpec((TM, d), lambda i: (i, 0)),
            pl.BlockSpec((TM, d), lambda i: (i, 0)),
            pl.BlockSpec(modp.shape, lambda i: (0, 0, 0)),
            pl.BlockSpec(mods.shape, lambda i: (0, 0, 0)),
        ],
        out_specs=pl.BlockSpec((TM, d), lambda i: (i, 0)),
        out_shape=jax.ShapeDtypeStruct((n_pad, d), F32),
        compiler_params=_params("parallel"),
        name="gated_residual",
    )(x, y, modp, mods)


def _rel_bucket(dist):
    n = jnp.maximum(dist, 0)
    max_exact = N_BUCKETS // 2
    nf = jnp.maximum(n, 1).astype(F32)
    large = max_exact + (jnp.log(nf / max_exact) / math.log(MAX_DISTANCE / max_exact)
                         * (N_BUCKETS - max_exact)).astype(jnp.int32)
    return jnp.where(n < max_exact, n, jnp.minimum(large, N_BUCKETS - 1))


def _bias_tiles(rel_bias):
    assert MOBA_BLOCK + 1 >= MAX_DISTANCE
    i = jnp.arange(MOBA_BLOCK)[:, None]
    p = jnp.arange(MOBA_BLOCK)[None, :]
    own = rel_bias[_rel_bucket(i - p)]
    prev = rel_bias[_rel_bucket(MOBA_BLOCK + i - p)]
    tiles = jnp.stack([own, prev], axis=0).transpose(3, 0, 1, 2)
    far = jnp.broadcast_to(rel_bias[N_BUCKETS - 1][:, None, None], (N_HEADS, SUBLANES, LANES))
    return tiles, far


def kernel(x_prompt, x_sample, c_prompt, c_sample, cache_k, cache_v, state_pool, page_table, rel_bias,
           w_mod, b_mod, norm1_w, w_in, q_norm_w, k_norm_w, pool_w, pool_scale, w_pa, w_pb, w_out,
           norm2_w, peer_wq, peer_sub_keys, peer_u, peer_v):
    n_batch, seq, d = x_prompt.shape
    db, t_new, _ = x_sample.shape
    depth = w_mod.shape[0]
    n_pages = page_table.shape[1]
    past_len = n_pages * PAGE_SIZE
    n_prompt = n_batch * seq
    n_samp = db * t_new
    n_past_blocks = past_len // MOBA_BLOCK
    assert past_len % MOBA_BLOCK == 0 and n_past_blocks >= MOBA_TOPK
    assert seq % TP == 0 and seq % MOBA_BLOCK == 0 and n_samp <= TM and n_samp % SUBLANES == 0
    assert t_new <= SUBLANES and n_batch <= SUBLANES and n_batch + db <= 16
    assert n_prompt % TM == 0
    step = math.lcm(TM, TT)
    n_pad = -(-(n_prompt + n_samp) // step) * step
    geom = dict(tiles_per_seq=seq // TM, n_ptiles=n_prompt // TM, n_batch=n_batch)

    off_ga, off_gb, off_u = 0, d, 2 * d
    off_q = off_u + POOL_WIDTH
    off_k = off_q + ATT_WIDTH
    off_v = off_k + ATT_WIDTH
    o_q = POOL_WIDTH
    o_ga = o_q + 3 * ATT_WIDTH
    w_in_bf = jnp.concatenate([w_in[:, :, o_ga:], w_in[:, :, :o_ga]], axis=-1).astype(BF16)
    w_pa_bf, w_pb_bf, w_out_bf = w_pa.astype(BF16), w_pb.astype(BF16), w_out.astype(BF16)
    wq_bf, u_bf, v_bf = peer_wq.astype(BF16), peer_u.astype(BF16), peer_v.astype(BF16)

    c16 = jnp.zeros((16, d), F32).at[:n_batch].set(c_prompt).at[n_batch:n_batch + db].set(c_sample)
    mod = _modulation(c16, w_mod, b_mod).reshape(depth, 16, 6, d).transpose(0, 2, 1, 3)
    modp_all = mod[:, :, :SUBLANES]
    mods_all = jnp.repeat(mod[:, :, n_batch:n_batch + db], t_new, axis=2)

    bias_tiles, far = _bias_tiles(rel_bias)
    cache_k4, cache_v4 = _head_major(cache_k), _head_major(cache_v)
    kmean = _cache_block_means(cache_k4, page_table).reshape(db, n_past_blocks, N_HEADS, depth, HEAD_DIM)

    x = jnp.zeros((n_pad, d), F32).at[:n_prompt].set(x_prompt.reshape(n_prompt, d))
    x = x.at[n_prompt:n_prompt + n_samp].set(x_sample.reshape(n_samp, d))

    k_rows, v_rows, pool_p, pool_s = [], [], [], []
    for l in range(depth):
        z = _mixer_in(x, modp_all[l, 0:2], mods_all[l, 0:2], norm1_w[l][None], w_in_bf[l],
                      q_norm_w[l][None], k_norm_w[l][None], geom)
        k_rows.append(z[:, off_k:off_k + ATT_WIDTH])
        v_rows.append(z[:, off_v:off_v + ATT_WIDTH])
        u_p = z[:n_prompt, off_u:off_u + POOL_WIDTH].reshape(n_batch, seq, POOL_WIDTH)
        pool_p.append(u_p[:, seq - POOL_STATE:])
        zs = z[n_prompt:n_prompt + n_samp].reshape(db, t_new, -1)
        u_s = zs[:, :, off_u:off_u + POOL_WIDTH]
        pool_s.append(jnp.concatenate([state_pool[l], u_s], axis=1)[:, -POOL_STATE:])

        ypool_p = _pool_prompt(z, pool_w[l], pool_scale[l][None], n_batch, seq, off_u)
        yatt_p = _moba_prompt(z, bias_tiles, far, n_batch, seq, off_q)

        zs8 = jnp.pad(zs, ((0, 0), (0, SUBLANES - t_new), (0, 0)))
        state16 = jnp.pad(state_pool[l], ((0, 0), (POOL_HALO - POOL_STATE, 0), (0, 0)))
        kmean_l = kmean[:, :, :, l].transpose(0, 2, 1, 3)
        ypool_s, idx = _sample_pre(zs8, state16, kmean_l, pool_w[l], pool_scale[l][None], past_len,
                                   off_u, off_q)
        sel = idx[:, :, :t_new, :MOBA_TOPK]
        ppb = MOBA_BLOCK // PAGE_SIZE
        logical = sel[..., None] * ppb + jnp.arange(ppb)
        pages = page_table[jnp.arange(db)[:, None, None, None, None], logical]
        yatt_s = _sample_att(pages.reshape(-1), sel.reshape(-1), zs8, bias_tiles, far, cache_k4, cache_v4,
                             l, depth, t_new, n_past_blocks, off_q)

        tail = n_pad - n_prompt - n_samp
        ypool = jnp.concatenate([ypool_p, ypool_s[:, :t_new].reshape(n_samp, POOL_WIDTH),
                                 jnp.zeros((tail, POOL_WIDTH), F32)], axis=0)
        yatt = jnp.concatenate([yatt_p, yatt_s[:, :t_new].reshape(n_samp, ATT_WIDTH),
                                jnp.zeros((tail, ATT_WIDTH), F32)], axis=0)
        x1 = _merge(ypool, yatt, z, x, modp_all[l, 2:3], mods_all[l, 2:3], w_pa_bf[l], w_pb_bf[l],
                    w_out_bf[l], geom)
        h2, eid, g = _route(x1, modp_all[l, 3:5], mods_all[l, 3:5], norm2_w[l][None], wq_bf[l],
                            peer_sub_keys[l], geom)
        y = _experts(h2, u_bf[l], v_bf[l], eid, g)
        x = _residual(x1, y, modp_all[l, 5:6], mods_all[l, 5:6], geom)

    def rows(parts, lo, hi, lead):
        return jnp.stack([p[lo:hi] for p in parts], axis=1).reshape(*lead, depth, N_HEADS, HEAD_DIM)

    y_prompt = x[:n_prompt].reshape(n_batch, seq, d)
    y_sample = x[n_prompt:n_prompt + n_samp].reshape(db, t_new, d)
    return (y_prompt, y_sample,
            rows(k_rows, 0, n_prompt, (n_batch, seq)), rows(v_rows, 0, n_prompt, (n_batch, seq)),
            jnp.stack(pool_p, axis=0),
            rows(k_rows, n_prompt, n_prompt + n_samp, (db, t_new)),
            rows(v_rows, n_prompt, n_prompt + n_samp, (db, t_new)),
            jnp.stack(pool_s, axis=0))
```

```python
import functools
import math

import jax
import jax.numpy as jnp
from jax import lax
from jax.experimental import pallas as pl
from jax.experimental.pallas import tpu as pltpu

F32 = jnp.float32
BF16 = jnp.bfloat16
HIGHEST = lax.Precision.HIGHEST

N_HEADS = 12
HEAD_DIM = 128
ATT_WIDTH = N_HEADS * HEAD_DIM
MOBA_BLOCK = 256
MOBA_TOPK = 3
ATT_SCALE = HEAD_DIM ** -0.5
N_BUCKETS = 32
MAX_DISTANCE = 128
POOL_WINDOWS = (2, 4, 8, 16)
POOL_GROUP_DIM = 128
POOL_WIDTH = len(POOL_WINDOWS) * POOL_GROUP_DIM
POOL_STATE = max(POOL_WINDOWS) - 1
POOL_HALO = 16
PEER_HEADS = 8
PEER_QDIM = 256
PEER_NKEYS = 128
PEER_TOPK = 16
PEER_SLOTS = PEER_HEADS * PEER_TOPK
N_EXPERTS = PEER_NKEYS * PEER_NKEYS
PAGE_SIZE = 128
EPS = 1e-6
NEG_INF = float("-inf")

LANES = 128
SUBLANES = 8
VMEM_LIMIT = 56 * 1024 * 1024

TM = 256
TN_IN = 1536
TN_OUT = 1024
TP = 512
QB = 256
TT = 768
TZ = 384
TE = 1024

_NT = (((1,), (1,)), ((), ()))
_TN = (((0,), (0,)), ((), ()))


def _params(*sem):
    return pltpu.CompilerParams(dimension_semantics=sem, vmem_limit_bytes=VMEM_LIMIT)


def _mod_kernel(c_ref, w_ref, b_ref, o_ref):
    c = c_ref[...]
    s = (c * jax.nn.sigmoid(c)).astype(BF16)
    o_ref[...] = jnp.dot(s, w_ref[...].astype(BF16), preferred_element_type=F32) + b_ref[...]


def _modulation(c16, w_mod, b_mod):
    depth, d, n6 = w_mod.shape
    tn = 1536
    return pl.pallas_call(
        _mod_kernel,
        grid=(depth, n6 // tn),
        in_specs=[
            pl.BlockSpec((16, d), lambda l, j: (0, 0)),
            pl.BlockSpec((None, d, tn), lambda l, j: (l, 0, j)),
            pl.BlockSpec((None, 1, tn), lambda l, j: (l, 0, j)),
        ],
        out_specs=pl.BlockSpec((None, 16, tn), lambda l, j: (l, 0, j)),
        out_shape=jax.ShapeDtypeStruct((depth, 16, n6), F32),
        compiler_params=_params("parallel", "parallel"),
        name="adaln_mod",
    )(c16, w_mod, b_mod.reshape(depth, 1, n6))


def _tile_mod(modp_ref, mods_ref, comp, i, tiles_per_seq, n_ptiles, n_batch, tm):
    b = jnp.minimum(i // tiles_per_seq, n_batch - 1)
    row = modp_ref[comp, pl.ds(b, 1), :]
    srows = mods_ref[comp]
    samp = jnp.concatenate([srows, jnp.zeros((tm - srows.shape[0], srows.shape[1]), F32)], axis=0)
    return jnp.where(i >= n_ptiles, samp, row)


def _mixer_in_kernel(x_ref, modp_ref, mods_ref, nw_ref, w_ref, qn_ref, kn_ref, o_ref, *,
                     tiles_per_seq, n_ptiles, n_batch, jq, jk):
    j = pl.program_id(0)
    i = pl.program_id(1)
    x = x_ref[...]
    n = x * lax.rsqrt(jnp.mean(x * x, axis=-1, keepdims=True) + EPS) * nw_ref[...]
    sh = _tile_mod(modp_ref, mods_ref, 0, i, tiles_per_seq, n_ptiles, n_batch, TM)
    sc = _tile_mod(modp_ref, mods_ref, 1, i, tiles_per_seq, n_ptiles, n_batch, TM)
    h = (n * (1.0 + sc) + sh).astype(BF16)
    z = jnp.dot(h, w_ref[...], preferred_element_type=F32)

    def head_norm(wn_ref):
        for hh in range(N_HEADS):
            sl = slice(hh * HEAD_DIM, (hh + 1) * HEAD_DIM)
            zg = z[:, sl]
            o_ref[:, sl] = zg * lax.rsqrt(jnp.mean(zg * zg, axis=-1, keepdims=True) + EPS) * wn_ref[...]

    @pl.when(j == jq)
    def _():
        head_norm(qn_ref)

    @pl.when(j == jk)
    def _():
        head_norm(kn_ref)

    @pl.when(jnp.logical_and(j != jq, j != jk))
    def _():
        o_ref[...] = z


def _mixer_in(x, modp, mods, nw, w_bf, qn, kn, geom):
    n_pad, d = x.shape
    in_w = w_bf.shape[1]
    off_q = 2 * d + POOL_WIDTH
    assert off_q % TN_IN == 0 and ATT_WIDTH == TN_IN
    kern = functools.partial(_mixer_in_kernel, jq=off_q // TN_IN, jk=off_q // TN_IN + 1, **geom)
    return pl.pallas_call(
        kern,
        grid=(in_w // TN_IN, n_pad // TM),
        in_specs=[
            pl.BlockSpec((TM, d), lambda j, i: (i, 0)),
            pl.BlockSpec(modp.shape, lambda j, i: (0, 0, 0)),
            pl.BlockSpec(mods.shape, lambda j, i: (0, 0, 0)),
            pl.BlockSpec((1, d), lambda j, i: (0, 0)),
            pl.BlockSpec((d, TN_IN), lambda j, i: (0, j)),
            pl.BlockSpec((1, HEAD_DIM), lambda j, i: (0, 0)),
            pl.BlockSpec((1, HEAD_DIM), lambda j, i: (0, 0)),
        ],
        out_specs=pl.BlockSpec((TM, TN_IN), lambda j, i: (i, j)),
        out_shape=jax.ShapeDtypeStruct((n_pad, in_w), F32),
        compiler_params=_params("parallel", "parallel"),
        name="mixer_in",
    )(x, modp, mods, nw, w_bf, qn, kn)


def _pool_compute(ext, pos_first, n_rows, pw_ref, ps_ref):
    rowpos = pos_first + lax.broadcasted_iota(jnp.int32, (n_rows, 1), 0)
    outs = []
    for g, w in enumerate(POOL_WINDOWS):
        sl = slice(g * POOL_GROUP_DIM, (g + 1) * POOL_GROUP_DIM)
        xg = ext[:, sl]
        s = xg
        k = 1
        while k < w:
            s = s + pltpu.roll(s, k, axis=0)
            k *= 2
        cnt = jnp.minimum(rowpos + 1, w).astype(F32)
        dlt = s[POOL_HALO:] / cnt - xg[POOL_HALO:]
        y = jnp.dot(dlt.astype(BF16), pw_ref[g].astype(BF16), preferred_element_type=F32)
        outs.append(y * ps_ref[:, sl])
    return jnp.concatenate(outs, axis=-1)


def _pool_prompt_kernel(cur_ref, halo_ref, pw_ref, ps_ref, o_ref):
    t = pl.program_id(1)
    halo = jnp.where(t == 0, 0.0, halo_ref[...])
    ext = jnp.concatenate([halo, cur_ref[...]], axis=0)
    o_ref[...] = _pool_compute(ext, t * TP, TP, pw_ref, ps_ref)


def _pool_prompt(z, pool_w, pool_scale, n_batch, seq, off_u):
    tps = seq // TP
    ub = off_u // POOL_WIDTH
    return pl.pallas_call(
        _pool_prompt_kernel,
        grid=(n_batch, tps),
        in_specs=[
            pl.BlockSpec((TP, POOL_WIDTH), lambda b, t: (b * tps + t, ub)),
            pl.BlockSpec((POOL_HALO, POOL_WIDTH),
                         lambda b, t: (jnp.maximum((b * seq + t * TP) // POOL_HALO - 1, 0), ub)),
            pl.BlockSpec(pool_w.shape, lambda b, t: (0, 0, 0)),
            pl.BlockSpec((1, POOL_WIDTH), lambda b, t: (0, 0)),
        ],
        out_specs=pl.BlockSpec((TP, POOL_WIDTH), lambda b, t: (b * tps + t, 0)),
        out_shape=jax.ShapeDtypeStruct((n_batch * seq, POOL_WIDTH), F32),
        compiler_params=_params("parallel", "parallel"),
        name="pool_prompt",
    )(z, z, pool_w, pool_scale)


def _top_blocks(gate, n_valid, n_blocks):
    col = lax.broadcasted_iota(jnp.int32, gate.shape, 1)
    g = jnp.where(col < n_valid, gate, NEG_INF)
    sels = []
    for r in range(MOBA_TOPK):
        m = jnp.max(g, axis=1, keepdims=True)
        idx = jnp.min(jnp.where(g == m, col, n_blocks), axis=1, keepdims=True)
        sels.append(jnp.where(r < n_valid, idx, -1))
        g = jnp.where(col == idx, NEG_INF, g)
    return sels


def _moba_prompt_kernel(q_ref, k_ref, v_ref, bias_ref, far_ref, o_ref, kmean_ref, kb_ref, vb_ref, *, n_blocks):
    qi = pl.program_id(2)
    blk = MOBA_BLOCK
    parts = blk // QB
    j = qi // parts
    r0 = pl.multiple_of((qi % parts) * QB, QB)

    @pl.when(qi == 0)
    def _():
        for n in range(n_blocks):
            kn = k_ref[n * blk:(n + 1) * blk, :]
            kmean_ref[n:n + 1, :] = jnp.sum(kn, axis=0, keepdims=True) * (1.0 / blk)
            kb_ref[n * blk:(n + 1) * blk, :] = kn.astype(BF16)
            vb_ref[n * blk:(n + 1) * blk, :] = v_ref[n * blk:(n + 1) * blk, :].astype(BF16)

    q = q_ref[...]
    gate = lax.dot_general(q, kmean_ref[...], _NT, precision=HIGHEST, preferred_element_type=F32)
    sels = _top_blocks(gate, j, n_blocks)
    qb = (q * ATT_SCALE).astype(BF16)
    far = far_ref[0:1, 0:1]

    def scores(n):
        kb = kb_ref[pl.ds(pl.multiple_of(n * blk, blk), blk), :]
        return lax.dot_general(qb, kb, _NT, preferred_element_type=F32)

    def values(n):
        return vb_ref[pl.ds(pl.multiple_of(n * blk, blk), blk), :]

    row = lax.broadcasted_iota(jnp.int32, (QB, blk), 0) + r0
    col = lax.broadcasted_iota(jnp.int32, (QB, blk), 1)
    s = jnp.where(col <= row, scores(j) + bias_ref[0, pl.ds(r0, QB), :], NEG_INF)
    m = jnp.max(s, axis=1, keepdims=True)
    p = jnp.exp(s - m)
    l = jnp.sum(p, axis=1, keepdims=True)
    acc = jnp.dot(p.astype(BF16), values(j), preferred_element_type=F32)

    def body(n, carry):
        m, l, acc = carry
        bias = jnp.where(n == j - 1, bias_ref[1, pl.ds(r0, QB), :], far)
        picked = (sels[0] == n) | (sels[1] == n) | (sels[2] == n)
        s = jnp.where(picked, scores(n) + bias, NEG_INF)
        m_new = jnp.maximum(m, jnp.max(s, axis=1, keepdims=True))
        alpha = jnp.exp(m - m_new)
        p = jnp.exp(s - m_new)
        l = alpha * l + jnp.sum(p, axis=1, keepdims=True)
        acc = alpha * acc + jnp.dot(p.astype(BF16), values(n), preferred_element_type=F32)
        return m_new, l, acc

    m, l, acc = lax.fori_loop(0, j, body, (m, l, acc))
    o_ref[...] = acc / l


def _moba_prompt(z, bias_tiles, far, n_batch, seq, off_q):
    n_blocks = seq // MOBA_BLOCK
    n_q = seq // QB
    qb0 = off_q // HEAD_DIM
    kb0 = qb0 + N_HEADS
    vb0 = kb0 + N_HEADS
    return pl.pallas_call(
        functools.partial(_moba_prompt_kernel, n_blocks=n_blocks),
        grid=(n_batch, N_HEADS, n_q),
        in_specs=[
            pl.BlockSpec((QB, HEAD_DIM), lambda b, h, j: (b * n_q + j, qb0 + h)),
            pl.BlockSpec((seq, HEAD_DIM), lambda b, h, j: (b, kb0 + h)),
            pl.BlockSpec((seq, HEAD_DIM), lambda b, h, j: (b, vb0 + h)),
            pl.BlockSpec((None, 2, MOBA_BLOCK, MOBA_BLOCK), lambda b, h, j: (h, 0, 0, 0)),
            pl.BlockSpec((None, SUBLANES, LANES), lambda b, h, j: (h, 0, 0)),
        ],
        out_specs=pl.BlockSpec((QB, HEAD_DIM), lambda b, h, j: (b * n_q + j, h)),
        out_shape=jax.ShapeDtypeStruct((n_batch * seq, ATT_WIDTH), F32),
        scratch_shapes=[pltpu.VMEM((n_blocks, HEAD_DIM), F32), pltpu.VMEM((seq, HEAD_DIM), BF16),
                        pltpu.VMEM((seq, HEAD_DIM), BF16)],
        compiler_params=_params("parallel", "parallel", "arbitrary"),
        name="moba_prompt",
    )(z, z, z, bias_tiles, far)


def _kmean_kernel(pt_ref, c_ref, o_ref, *, pages_per_block):
    p = pl.program_id(1)
    s = jnp.sum(c_ref[...], axis=0)

    @pl.when(p % pages_per_block == 0)
    def _():
        o_ref[...] = s

    @pl.when(p % pages_per_block != 0)
    def _():
        o_ref[...] = o_ref[...] + s

    @pl.when(p % pages_per_block == pages_per_block - 1)
    def _():
        o_ref[...] = o_ref[...] * (1.0 / MOBA_BLOCK)


def _head_major(cache):
    n_pool, page, depth, heads, hd = cache.shape
    return cache.transpose(0, 1, 3, 2, 4).reshape(n_pool, page, heads * depth, hd)


def _cache_block_means(cache_k4, page_table):
    n_pool, page, rows, hd = cache_k4.shape
    db, n_pages = page_table.shape
    ppb = MOBA_BLOCK // page
    grid_spec = pltpu.PrefetchScalarGridSpec(
        num_scalar_prefetch=1,
        grid=(db, n_pages),
        in_specs=[pl.BlockSpec((None, page, rows, hd), lambda b, p, pt: (pt[b * n_pages + p], 0, 0, 0))],
        out_specs=pl.BlockSpec((None, None, rows, hd), lambda b, p, pt: (b, p // ppb, 0, 0)),
    )
    return pl.pallas_call(
        functools.partial(_kmean_kernel, pages_per_block=ppb),
        grid_spec=grid_spec,
        out_shape=jax.ShapeDtypeStruct((db, n_pages // ppb, rows, hd), F32),
        compiler_params=_params("parallel", "arbitrary"),
        name="cache_block_means",
    )(page_table.reshape(-1), cache_k4)


def _sample_pre_kernel(u_ref, st_ref, q_ref, km_ref, pw_ref, ps_ref, yp_ref, idx_ref, *,
                       past_len, n_blocks):
    ext = jnp.concatenate([st_ref[...], u_ref[...]], axis=0)
    yp_ref[...] = _pool_compute(ext, past_len, SUBLANES, pw_ref, ps_ref)
    lane = lax.broadcasted_iota(jnp.int32, (SUBLANES, LANES), 1)
    for h in range(N_HEADS):
        qh = q_ref[:, h * HEAD_DIM:(h + 1) * HEAD_DIM]
        gate = lax.dot_general(qh, km_ref[h], _NT, precision=HIGHEST, preferred_element_type=F32)
        sels = _top_blocks(gate, n_blocks, n_blocks)
        out = jnp.zeros((SUBLANES, LANES), jnp.int32)
        for r in range(MOBA_TOPK):
            out = jnp.where(lane == r, sels[r], out)
        idx_ref[h] = out


def _sample_pre(zs8, state16, kmean_l, pool_w, pool_scale, past_len, off_u, off_q):
    db = zs8.shape[0]
    n_blocks = kmean_l.shape[2]
    return pl.pallas_call(
        functools.partial(_sample_pre_kernel, past_len=past_len, n_blocks=n_blocks),
        grid=(db,),
        in_specs=[
            pl.BlockSpec((None, SUBLANES, POOL_WIDTH), lambda b: (b, 0, off_u // POOL_WIDTH)),
            pl.BlockSpec((None, POOL_HALO, POOL_WIDTH), lambda b: (b, 0, 0)),
            pl.BlockSpec((None, SUBLANES, ATT_WIDTH), lambda b: (b, 0, off_q // ATT_WIDTH)),
            pl.BlockSpec((None, N_HEADS, n_blocks, HEAD_DIM), lambda b: (b, 0, 0, 0)),
            pl.BlockSpec(pool_w.shape, lambda b: (0, 0, 0)),
            pl.BlockSpec((1, POOL_WIDTH), lambda b: (0, 0)),
        ],
        out_specs=[
            pl.BlockSpec((None, SUBLANES, POOL_WIDTH), lambda b: (b, 0, 0)),
            pl.BlockSpec((None, N_HEADS, SUBLANES, LANES), lambda b: (b, 0, 0, 0)),
        ],
        out_shape=[
            jax.ShapeDtypeStruct((db, SUBLANES, POOL_WIDTH), F32),
            jax.ShapeDtypeStruct((db, N_HEADS, SUBLANES, LANES), jnp.int32),
        ],
        compiler_params=_params("parallel"),
        name="sample_pre",
    )(zs8, state16, zs8, kmean_l, pool_w, pool_scale)


def _sample_att_kernel(pages_ref, sel_ref, q_ref, kn_ref, vn_ref, bias_ref, far_ref, ck_ref, cv_ref,
                       o_ref, kbuf, vbuf, sem, *, layer, depth, t_new, last_block, pages_per_block):
    b = pl.program_id(0)
    h = pl.program_id(1)
    base = (b * N_HEADS + h) * t_new * MOBA_TOPK
    lh = h * depth + layer

    def copies(t, r, pg):
        page = pages_ref[(base + t * MOBA_TOPK + r) * pages_per_block + pg]
        dst = pl.ds(pg * PAGE_SIZE, PAGE_SIZE)
        return (pltpu.make_async_copy(ck_ref.at[page, :, lh, :], kbuf.at[t, r, dst, :], sem.at[0]),
                pltpu.make_async_copy(cv_ref.at[page, :, lh, :], vbuf.at[t, r, dst, :], sem.at[1]))

    slots = [(t, r, pg) for t in range(t_new) for r in range(MOBA_TOPK) for pg in range(pages_per_block)]
    for s in slots:
        for c in copies(*s):
            c.start()
    for s in slots:
        for c in copies(*s):
            c.wait()

    q = q_ref[...]
    qb = q.astype(BF16)
    far = far_ref[0:1, 0:1]
    row = lax.broadcasted_iota(jnp.int32, (SUBLANES, SUBLANES), 0)
    col = lax.broadcasted_iota(jnp.int32, (SUBLANES, SUBLANES), 1)
    s_own = lax.dot_general(qb, kn_ref[...].astype(BF16), _NT, preferred_element_type=F32) * ATT_SCALE
    s_own = jnp.where(jnp.logical_and(col <= row, col < t_new), s_own + bias_ref[0, :, 0:SUBLANES], NEG_INF)
    vn = vn_ref[...].astype(BF16)
    orow = lax.broadcasted_iota(jnp.int32, (SUBLANES, HEAD_DIM), 0)
    out = jnp.zeros((SUBLANES, HEAD_DIM), F32)
    n_sel = MOBA_TOPK * MOBA_BLOCK
    for t in range(t_new):
        kt = kbuf[t].reshape(n_sel, HEAD_DIM).astype(BF16)
        vt = vbuf[t].reshape(n_sel, HEAD_DIM).astype(BF16)
        s_sel = lax.dot_general(qb, kt, _NT, preferred_element_type=F32) * ATT_SCALE
        bias = jnp.concatenate(
            [jnp.where(sel_ref[base + t * MOBA_TOPK + r] == last_block, bias_ref[1],
                       jnp.broadcast_to(far, (SUBLANES, MOBA_BLOCK))) for r in range(MOBA_TOPK)], axis=1)
        s_sel = s_sel + bias
        m = jnp.maximum(jnp.max(s_sel, axis=1, keepdims=True), jnp.max(s_own, axis=1, keepdims=True))
        p_sel = jnp.exp(s_sel - m)
        p_own = jnp.exp(s_own - m)
        l = jnp.sum(p_sel, axis=1, keepdims=True) + jnp.sum(p_own, axis=1, keepdims=True)
        o_t = (jnp.dot(p_sel.astype(BF16), vt, preferred_element_type=F32)
               + jnp.dot(p_own.astype(BF16), vn, preferred_element_type=F32)) / l
        out = jnp.where(orow == t, o_t, out)
    o_ref[...] = out


def _sample_att(pages, sel, zs8, bias_tiles, far, cache_k4, cache_v4, layer, depth, t_new, n_blocks, off_q):
    db = zs8.shape[0]
    page = cache_k4.shape[1]
    ppb = MOBA_BLOCK // page
    qb0 = off_q // HEAD_DIM
    kb0 = qb0 + N_HEADS
    vb0 = kb0 + N_HEADS
    grid_spec = pltpu.PrefetchScalarGridSpec(
        num_scalar_prefetch=2,
        grid=(db, N_HEADS),
        in_specs=[
            pl.BlockSpec((None, SUBLANES, HEAD_DIM), lambda b, h, pg, sl: (b, 0, qb0 + h)),
            pl.BlockSpec((None, SUBLANES, HEAD_DIM), lambda b, h, pg, sl: (b, 0, kb0 + h)),
            pl.BlockSpec((None, SUBLANES, HEAD_DIM), lambda b, h, pg, sl: (b, 0, vb0 + h)),
            pl.BlockSpec((None, 2, SUBLANES, MOBA_BLOCK), lambda b, h, pg, sl: (h, 0, 0, 0)),
            pl.BlockSpec((None, SUBLANES, LANES), lambda b, h, pg, sl: (h, 0, 0)),
            pl.BlockSpec(memory_space=pl.ANY),
            pl.BlockSpec(memory_space=pl.ANY),
        ],
        out_specs=pl.BlockSpec((None, SUBLANES, HEAD_DIM), lambda b, h, pg, sl: (b, 0, h)),
        scratch_shapes=[
            pltpu.VMEM((t_new, MOBA_TOPK, MOBA_BLOCK, HEAD_DIM), F32),
            pltpu.VMEM((t_new, MOBA_TOPK, MOBA_BLOCK, HEAD_DIM), F32),
            pltpu.SemaphoreType.DMA((2,)),
        ],
    )
    return pl.pallas_call(
        functools.partial(_sample_att_kernel, layer=layer, depth=depth, t_new=t_new,
                          last_block=n_blocks - 1, pages_per_block=ppb),
        grid_spec=grid_spec,
        out_shape=jax.ShapeDtypeStruct((db, SUBLANES, ATT_WIDTH), F32),
        compiler_params=_params("arbitrary", "arbitrary"),
        name="sample_att",
    )(pages, sel, zs8, zs8, zs8, bias_tiles, far, cache_k4, cache_v4)


def _merge_kernel(yp_ref, ya_ref, ga_ref, gb_ref, x_ref, modp_ref, mods_ref, wpa_ref, wpb_ref, wo_ref,
                  o_ref, m_sc, *, tiles_per_seq, n_ptiles, n_batch):
    i = pl.program_id(0)
    j = pl.program_id(1)

    @pl.when(j == 0)
    def _():
        a = jnp.dot(yp_ref[...].astype(BF16), wpa_ref[...], preferred_element_type=F32)
        c = jnp.dot(ya_ref[...].astype(BF16), wpb_ref[...], preferred_element_type=F32)
        m = jax.nn.sigmoid(ga_ref[...]) * a + jax.nn.sigmoid(gb_ref[...]) * c
        m_sc[...] = m.astype(BF16)

    g1 = _tile_mod(modp_ref, mods_ref, 0, i, tiles_per_seq, n_ptiles, n_batch, TM)
    r = jnp.dot(m_sc[...], wo_ref[...], preferred_element_type=F32)
    o_ref[...] = x_ref[...] + g1 * r


def _merge(ypool, yatt, z, x, modp, mods, wpa, wpb, wo, geom):
    n_pad, d = x.shape
    return pl.pallas_call(
        functools.partial(_merge_kernel, **geom),
        grid=(n_pad // TM, d // TN_OUT),
        in_specs=[
            pl.BlockSpec((TM, POOL_WIDTH), lambda i, j: (i, 0)),
            pl.BlockSpec((TM, ATT_WIDTH), lambda i, j: (i, 0)),
            pl.BlockSpec((TM, d), lambda i, j: (i, 0)),
            pl.BlockSpec((TM, d), lambda i, j: (i, 1)),
            pl.BlockSpec((TM, TN_OUT), lambda i, j: (i, j)),
            pl.BlockSpec((1, SUBLANES, TN_OUT), lambda i, j: (0, 0, j)),
            pl.BlockSpec((1, mods.shape[1], TN_OUT), lambda i, j: (0, 0, j)),
            pl.BlockSpec(wpa.shape, lambda i, j: (0, 0)),
            pl.BlockSpec(wpb.shape, lambda i, j: (0, 0)),
            pl.BlockSpec((d, TN_OUT), lambda i, j: (0, j)),
        ],
        out_specs=pl.BlockSpec((TM, TN_OUT), lambda i, j: (i, j)),
        out_shape=jax.ShapeDtypeStruct((n_pad, d), F32),
        scratch_shapes=[pltpu.VMEM((TM, d), BF16)],
        compiler_params=_params("parallel", "arbitrary"),
        name="mixer_merge",
    )(ypool, yatt, z, z, x, modp, mods, wpa, wpb, wo)


def _top16_rows(xs):
    t = xs[0].shape[1]
    rids = [lax.broadcasted_iota(jnp.int32, x.shape, 0) for x in xs]
    r16 = lax.broadcasted_iota(jnp.int32, (PEER_TOPK, t), 0)

    def body(k, carry):
        out = []
        for (x, vals, idxs), rid in zip(carry, rids):
            m = jnp.max(x, axis=0, keepdims=True)
            i = jnp.min(jnp.where(x == m, rid, x.shape[0]), axis=0, keepdims=True)
            out.append((jnp.where(rid == i, NEG_INF, x), jnp.where(r16 == k, m, vals),
                        jnp.where(r16 == k, i, idxs)))
        return tuple(out)

    init = tuple((x, jnp.zeros((PEER_TOPK, t), F32), jnp.zeros((PEER_TOPK, t), jnp.int32)) for x in xs)
    res = lax.fori_loop(0, PEER_TOPK, body, init)
    return [(v, i) for _, v, i in res]


_PAIR_COUNT = [PEER_TOPK // (a + 1) for a in range(PEER_TOPK)]
_PAIR_START = [sum(_PAIR_COUNT[:a]) for a in range(PEER_TOPK)]
_N_PAIRS = sum(_PAIR_COUNT)
_PAIR_ROWS = -(-_N_PAIRS // SUBLANES) * SUBLANES


def _joint_top(v1, i1, v2, i2):
    t = v1.shape[1]
    rid = lax.broadcasted_iota(jnp.int32, (_PAIR_ROWS, t), 0)
    c1 = jnp.full((_PAIR_ROWS, t), NEG_INF, F32)
    k1 = jnp.zeros((_PAIR_ROWS, t), jnp.int32)
    brow = jnp.full((_PAIR_ROWS, t), -1, jnp.int32)
    for a in range(PEER_TOPK):
        in_a = jnp.logical_and(rid >= _PAIR_START[a], rid < _PAIR_START[a] + _PAIR_COUNT[a])
        c1 = jnp.where(in_a, v1[a:a + 1, :], c1)
        k1 = jnp.where(in_a, i1[a:a + 1, :], k1)
        brow = jnp.where(in_a, rid - _PAIR_START[a], brow)
    c2 = jnp.zeros((_PAIR_ROWS, t), F32)
    k2 = jnp.zeros((_PAIR_ROWS, t), jnp.int32)
    for b in range(PEER_TOPK):
        c2 = jnp.where(brow == b, v2[b:b + 1, :], c2)
        k2 = jnp.where(brow == b, i2[b:b + 1, :], k2)
    (best, pos), = _top16_rows([c1 + c2])
    e1 = jnp.zeros_like(pos)
    e2 = jnp.zeros_like(pos)
    for r in range(_N_PAIRS):
        e1 = jnp.where(pos == r, k1[r:r + 1, :], e1)
        e2 = jnp.where(pos == r, k2[r:r + 1, :], e2)
    return best, e1, e2


def _route_kernel(x_ref, modp_ref, mods_ref, nw_ref, wq_ref, sk_ref, h_ref, eid_ref, g_ref, hq_sc, *,
                  tiles_per_seq, n_ptiles, n_batch):
    i = pl.program_id(0)
    x = x_ref[...]
    n = x * lax.rsqrt(jnp.mean(x * x, axis=-1, keepdims=True) + EPS) * nw_ref[...]
    sh = _tile_mod(modp_ref, mods_ref, 0, i, tiles_per_seq, n_ptiles, n_batch, TM)
    sc = _tile_mod(modp_ref, mods_ref, 1, i, tiles_per_seq, n_ptiles, n_batch, TM)
    hb = (n * (1.0 + sc) + sh).astype(BF16)
    h_ref[...] = hb
    hq_sc[...] = jnp.dot(hb, wq_ref[...], preferred_element_type=F32)
    half = PEER_QDIM // 2
    for part in range(TM // LANES):
        rows = slice(part * LANES, (part + 1) * LANES)
        q1 = jnp.concatenate([hq_sc[rows, hd * PEER_QDIM:hd * PEER_QDIM + half] for hd in range(PEER_HEADS)], 0)
        q2 = jnp.concatenate([hq_sc[rows, hd * PEER_QDIM + half:(hd + 1) * PEER_QDIM]
                              for hd in range(PEER_HEADS)], 0)
        s1_all = lax.dot_general(sk_ref[0], q1, _NT, precision=HIGHEST, preferred_element_type=F32)
        s2_all = lax.dot_general(sk_ref[1], q2, _NT, precision=HIGHEST, preferred_element_type=F32)
        eids, gs = [], []
        for hd in range(PEER_HEADS):
            cols = slice(hd * LANES, (hd + 1) * LANES)
            (v1, i1), (v2, i2) = _top16_rows([s1_all[:, cols], s2_all[:, cols]])
            best, e1, e2 = _joint_top(v1, i1, v2, i2)
            ex = jnp.exp(best - jnp.max(best, axis=0, keepdims=True))
            eids.append(e1 * PEER_NKEYS + e2)
            gs.append(ex / jnp.sum(ex, axis=0, keepdims=True))
        eid_ref[rows, :] = jnp.concatenate(eids, axis=0).T
        g_ref[rows, :] = jnp.concatenate(gs, axis=0).T


def _route(x, modp, mods, nw, wq, sub_keys, geom):
    n_pad, d = x.shape
    return pl.pallas_call(
        functools.partial(_route_kernel, **geom),
        grid=(n_pad // TM,),
        in_specs=[
            pl.BlockSpec((TM, d), lambda i: (i, 0)),
            pl.BlockSpec(modp.shape, lambda i: (0, 0, 0)),
            pl.BlockSpec(mods.shape, lambda i: (0, 0, 0)),
            pl.BlockSpec((1, d), lambda i: (0, 0)),
            pl.BlockSpec(wq.shape, lambda i: (0, 0)),
            pl.BlockSpec(sub_keys.shape, lambda i: (0, 0, 0)),
        ],
        out_specs=[
            pl.BlockSpec((TM, d), lambda i: (i, 0)),
            pl.BlockSpec((TM, PEER_SLOTS), lambda i: (i, 0)),
            pl.BlockSpec((TM, PEER_SLOTS), lambda i: (i, 0)),
        ],
        out_shape=[
            jax.ShapeDtypeStruct((n_pad, d), BF16),
            jax.ShapeDtypeStruct((n_pad, PEER_SLOTS), jnp.int32),
            jax.ShapeDtypeStruct((n_pad, PEER_SLOTS), F32),
        ],
        scratch_shapes=[pltpu.VMEM((TM, wq.shape[1]), F32)],
        compiler_params=_params("parallel"),
        name="peer_route",
    )(x, modp, mods, nw, wq, sub_keys)


def _gelu(a):
    return 0.5 * a * (1.0 + lax.erf(a * (1.0 / math.sqrt(2.0))))


def _expert_act_kernel(h_ref, u_ref, eid_ref, g_ref, z_ref):
    j = pl.program_id(1)
    rows_per_tile = TE // PEER_NKEYS
    a = lax.dot_general(h_ref[...], u_ref[...], _NT, preferred_element_type=F32)
    eid = eid_ref[...]
    k1 = eid // PEER_NKEYS
    k2 = eid % PEER_NKEYS

    @pl.when(j == 0)
    def _():
        z_ref[...] = jnp.zeros_like(z_ref)

    act = z_ref[...]
    for rl in range(rows_per_tile):
        picked = jnp.take_along_axis(a[:, rl * PEER_NKEYS:(rl + 1) * PEER_NKEYS], k2, axis=1)
        act = jnp.where(k1 == j * rows_per_tile + rl, picked, act)
    z_ref[...] = act

    @pl.when(j == pl.num_programs(1) - 1)
    def _():
        z_ref[...] = g_ref[...] * _gelu(act)


def _expert_act(h2, u_bf, eid, g):
    n_pad, d = h2.shape
    n_exp = u_bf.shape[0]
    slot_block = pl.BlockSpec((TT, PEER_SLOTS), lambda i, j: (i, 0))
    return pl.pallas_call(
        _expert_act_kernel,
        grid=(n_pad // TT, n_exp // TE),
        in_specs=[
            pl.BlockSpec((TT, d), lambda i, j: (i, 0)),
            pl.BlockSpec((TE, d), lambda i, j: (j, 0)),
            slot_block, slot_block,
        ],
        out_specs=slot_block,
        out_shape=jax.ShapeDtypeStruct((n_pad, PEER_SLOTS), F32),
        compiler_params=_params("parallel", "arbitrary"),
        name="peer_act",
    )(h2, u_bf, eid, g)


def _expert_out_kernel(eid_ref, z_ref, v_ref, o_ref, zall_sc):
    j = pl.program_id(1)
    rows_per_tile = TE // PEER_NKEYS
    n_groups = PEER_NKEYS // rows_per_tile

    @pl.when(j == 0)
    def _():
        sub = lax.broadcasted_iota(jnp.int32, (PEER_NKEYS, PEER_SLOTS), 0)

        def token(t, carry):
            eid = eid_ref[pl.ds(t, 1), :]
            zrow = z_ref[pl.ds(t, 1), :]
            lhs = jnp.where(eid // PEER_NKEYS == sub, zrow, 0.0).astype(BF16)
            rhs = jnp.where(eid % PEER_NKEYS == sub, 1.0, 0.0).astype(BF16)
            zt = lax.dot_general(lhs, rhs, _NT, preferred_element_type=F32)
            base = pl.multiple_of(t * rows_per_tile, rows_per_tile)
            for gi in range(n_groups):
                zall_sc[gi, pl.ds(base, rows_per_tile), :] = zt[gi * rows_per_tile:(gi + 1) * rows_per_tile, :]
            return carry

        lax.fori_loop(0, TZ, token, 0, unroll=4)

    y = None
    for pair in range(rows_per_tile // 2):
        zr = jnp.concatenate(
            [zall_sc[j, pl.ds(2 * pair + r, TZ, stride=rows_per_tile), :] for r in range(2)], axis=1)
        part = jnp.dot(zr.astype(BF16), v_ref[2 * pair * PEER_NKEYS:(2 * pair + 2) * PEER_NKEYS, :],
                       preferred_element_type=F32)
        y = part if y is None else y + part

    @pl.when(j == 0)
    def _():
        o_ref[...] = y

    @pl.when(j != 0)
    def _():
        o_ref[...] = o_ref[...] + y


def _expert_out(eid, z, v_bf):
    n_pad = eid.shape[0]
    n_exp, d = v_bf.shape
    rows_per_tile = TE // PEER_NKEYS
    assert rows_per_tile == SUBLANES
    slot_block = pl.BlockSpec((TZ, PEER_SLOTS), lambda i, j: (i, 0))
    return pl.pallas_call(
        _expert_out_kernel,
        grid=(n_pad // TZ, n_exp // TE),
        in_specs=[slot_block, slot_block, pl.BlockSpec((TE, d), lambda i, j: (j, 0))],
        out_specs=pl.BlockSpec((TZ, d), lambda i, j: (i, 0)),
        out_shape=jax.ShapeDtypeStruct((n_pad, d), F32),
        scratch_shapes=[pltpu.VMEM((PEER_NKEYS // rows_per_tile, TZ * rows_per_tile, PEER_NKEYS), F32)],
        compiler_params=_params("parallel", "arbitrary"),
        name="peer_out",
    )(eid, z, v_bf)


def _residual_kernel(x_ref, y_ref, modp_ref, mods_ref, o_ref, *, tiles_per_seq, n_ptiles, n_batch):
    i = pl.program_id(0)
    g2 = _tile_mod(modp_ref, mods_ref, 0, i, tiles_per_seq, n_ptiles, n_batch, TM)
    o_ref[...] = x_ref[...] + g2 * y_ref[...]


def _residual(x, y, modp, mods, geom):
    n_pad, d = x.shape
    return pl.pallas_call(
        functools.partial(_residual_kernel, **geom),
        grid=(n_pad // TM,),
        in_specs=[
            pl.BlockSpec((TM, d), lambda i: (i, 0)),
            pl.BlockSpec((TM, d), lambda i: (i, 0)),
            pl.BlockSpec(modp.shape, lambda i: (0, 0, 0)),
            pl.BlockSpec(mods.shape, lambda i: (0, 0, 0)),
        ],
        out_specs=pl.BlockSpec((TM, d), lambda i: (i, 0)),
        out_shape=jax.ShapeDtypeStruct((n_pad, d), F32),
        compiler_params=_params("parallel"),
        name="gated_residual",
    )(x, y, modp, mods)


def _rel_bucket(dist):
    n = jnp.maximum(dist, 0)
    max_exact = N_BUCKETS // 2
    nf = jnp.maximum(n, 1).astype(F32)
    large = max_exact + (jnp.log(nf / max_exact) / math.log(MAX_DISTANCE / max_exact)
                         * (N_BUCKETS - max_exact)).astype(jnp.int32)
    return jnp.where(n < max_exact, n, jnp.minimum(large, N_BUCKETS - 1))


def _bias_tiles(rel_bias):
    assert MOBA_BLOCK + 1 >= MAX_DISTANCE
    i = jnp.arange(MOBA_BLOCK)[:, None]
    p = jnp.arange(MOBA_BLOCK)[None, :]
    own = rel_bias[_rel_bucket(i - p)]
    prev = rel_bias[_rel_bucket(MOBA_BLOCK + i - p)]
    tiles = jnp.stack([own, prev], axis=0).transpose(3, 0, 1, 2)
    far = jnp.broadcast_to(rel_bias[N_BUCKETS - 1][:, None, None], (N_HEADS, SUBLANES, LANES))
    return tiles, far


def kernel(x_prompt, x_sample, c_prompt, c_sample, cache_k, cache_v, state_pool, page_table, rel_bias,
           w_mod, b_mod, norm1_w, w_in, q_norm_w, k_norm_w, pool_w, pool_scale, w_pa, w_pb, w_out,
           norm2_w, peer_wq, peer_sub_keys, peer_u, peer_v):
    n_batch, seq, d = x_prompt.shape
    db, t_new, _ = x_sample.shape
    depth = w_mod.shape[0]
    n_pages = page_table.shape[1]
    past_len = n_pages * PAGE_SIZE
    n_prompt = n_batch * seq
    n_samp = db * t_new
    n_past_blocks = past_len // MOBA_BLOCK
    assert past_len % MOBA_BLOCK == 0 and n_past_blocks >= MOBA_TOPK
    assert seq % TP == 0 and seq % MOBA_BLOCK == 0 and n_samp <= TM and n_samp % SUBLANES == 0
    assert PEER_SLOTS == PEER_NKEYS and MOBA_BLOCK % QB == 0 and TM % LANES == 0
    assert t_new <= SUBLANES and n_batch <= SUBLANES and n_batch + db <= 16
    assert n_prompt % TM == 0
    step = math.lcm(TM, TT, TZ)
    n_pad = -(-(n_prompt + n_samp) // step) * step
    geom = dict(tiles_per_seq=seq // TM, n_ptiles=n_prompt // TM, n_batch=n_batch)

    off_ga, off_gb, off_u = 0, d, 2 * d
    off_q = off_u + POOL_WIDTH
    off_k = off_q + ATT_WIDTH
    off_v = off_k + ATT_WIDTH
    o_q = POOL_WIDTH
    o_ga = o_q + 3 * ATT_WIDTH
    w_in_bf = jnp.concatenate([w_in[:, :, o_ga:], w_in[:, :, :o_ga]], axis=-1).astype(BF16)
    w_pa_bf, w_pb_bf, w_out_bf = w_pa.astype(BF16), w_pb.astype(BF16), w_out.astype(BF16)
    wq_bf, u_bf, v_bf = peer_wq.astype(BF16), peer_u.astype(BF16), peer_v.astype(BF16)

    c16 = jnp.zeros((16, d), F32).at[:n_batch].set(c_prompt).at[n_batch:n_batch + db].set(c_sample)
    mod = _modulation(c16, w_mod, b_mod).reshape(depth, 16, 6, d).transpose(0, 2, 1, 3)
    modp_all = mod[:, :, :SUBLANES]
    mods_all = jnp.repeat(mod[:, :, n_batch:n_batch + db], t_new, axis=2)

    bias_tiles, far = _bias_tiles(rel_bias)
    cache_k4, cache_v4 = _head_major(cache_k), _head_major(cache_v)
    kmean = _cache_block_means(cache_k4, page_table).reshape(db, n_past_blocks, N_HEADS, depth, HEAD_DIM)

    x = jnp.zeros((n_pad, d), F32).at[:n_prompt].set(x_prompt.reshape(n_prompt, d))
    x = x.at[n_prompt:n_prompt + n_samp].set(x_sample.reshape(n_samp, d))

    k_rows, v_rows, pool_p, pool_s = [], [], [], []
    for l in range(depth):
        z = _mixer_in(x, modp_all[l, 0:2], mods_all[l, 0:2], norm1_w[l][None], w_in_bf[l],
                      q_norm_w[l][None], k_norm_w[l][None], geom)
        k_rows.append(z[:, off_k:off_k + ATT_WIDTH])
        v_rows.append(z[:, off_v:off_v + ATT_WIDTH])
        u_p = z[:n_prompt, off_u:off_u + POOL_WIDTH].reshape(n_batch, seq, POOL_WIDTH)
        pool_p.append(u_p[:, seq - POOL_STATE:])
        zs = z[n_prompt:n_prompt + n_samp].reshape(db, t_new, -1)
        u_s = zs[:, :, off_u:off_u + POOL_WIDTH]
        pool_s.append(jnp.concatenate([state_pool[l], u_s], axis=1)[:, -POOL_STATE:])

        ypool_p = _pool_prompt(z, pool_w[l], pool_scale[l][None], n_batch, seq, off_u)
        yatt_p = _moba_prompt(z, bias_tiles, far, n_batch, seq, off_q)

        zs8 = jnp.pad(zs, ((0, 0), (0, SUBLANES - t_new), (0, 0)))
        state16 = jnp.pad(state_pool[l], ((0, 0), (POOL_HALO - POOL_STATE, 0), (0, 0)))
        kmean_l = kmean[:, :, :, l].transpose(0, 2, 1, 3)
        ypool_s, idx = _sample_pre(zs8, state16, kmean_l, pool_w[l], pool_scale[l][None], past_len,
                                   off_u, off_q)
        sel = idx[:, :, :t_new, :MOBA_TOPK]
        ppb = MOBA_BLOCK // PAGE_SIZE
        logical = sel[..., None] * ppb + jnp.arange(ppb)
        pages = page_table[jnp.arange(db)[:, None, None, None, None], logical]
        yatt_s = _sample_att(pages.reshape(-1), sel.reshape(-1), zs8, bias_tiles, far, cache_k4, cache_v4,
                             l, depth, t_new, n_past_blocks, off_q)

        tail = n_pad - n_prompt - n_samp
        ypool = jnp.concatenate([ypool_p, ypool_s[:, :t_new].reshape(n_samp, POOL_WIDTH),
                                 jnp.zeros((tail, POOL_WIDTH), F32)], axis=0)
        yatt = jnp.concatenate([yatt_p, yatt_s[:, :t_new].reshape(n_samp, ATT_WIDTH),
                                jnp.zeros((tail, ATT_WIDTH), F32)], axis=0)
        x1 = _merge(ypool, yatt, z, x, modp_all[l, 2:3], mods_all[l, 2:3], w_pa_bf[l], w_pb_bf[l],
                    w_out_bf[l], geom)
        h2, eid, g = _route(x1, modp_all[l, 3:5], mods_all[l, 3:5], norm2_w[l][None], wq_bf[l],
                            peer_sub_keys[l], geom)
        y = _expert_out(eid, _expert_act(h2, u_bf[l], eid, g), v_bf[l])
        x = _residual(x1, y, modp_all[l, 5:6], mods_all[l, 5:6], geom)

    def rows(parts, lo, hi, lead):
        return jnp.stack([p[lo:hi] for p in parts], axis=1).reshape(*lead, depth, N_HEADS, HEAD_DIM)

    y_prompt = x[:n_prompt].reshape(n_batch, seq, d)
    y_sample = x[n_prompt:n_prompt + n_samp].reshape(db, t_new, d)
    return (y_prompt, y_sample,
            rows(k_rows, 0, n_prompt, (n_batch, seq)), rows(v_rows, 0, n_prompt, (n_batch, seq)),
            jnp.stack(pool_p, axis=0),
            rows(k_rows, n_prompt, n_prompt + n_samp, (db, t_new)),
            rows(v_rows, n_prompt, n_prompt + n_samp, (db, t_new)),
            jnp.stack(pool_s, axis=0))
```

```python
import functools
import math

import jax
import jax.numpy as jnp
from jax import lax
from jax.experimental import pallas as pl
from jax.experimental.pallas import tpu as pltpu

F32 = jnp.float32
BF16 = jnp.bfloat16
HIGHEST = lax.Precision.HIGHEST

N_HEADS = 12
HEAD_DIM = 128
ATT_WIDTH = N_HEADS * HEAD_DIM
MOBA_BLOCK = 256
MOBA_TOPK = 3
ATT_SCALE = HEAD_DIM ** -0.5
N_BUCKETS = 32
MAX_DISTANCE = 128
POOL_WINDOWS = (2, 4, 8, 16)
POOL_GROUP_DIM = 128
POOL_WIDTH = len(POOL_WINDOWS) * POOL_GROUP_DIM
POOL_STATE = max(POOL_WINDOWS) - 1
POOL_HALO = 16
PEER_HEADS = 8
PEER_QDIM = 256
PEER_NKEYS = 128
PEER_TOPK = 16
PEER_SLOTS = PEER_HEADS * PEER_TOPK
N_EXPERTS = PEER_NKEYS * PEER_NKEYS
PAGE_SIZE = 128
EPS = 1e-6
NEG_INF = float("-inf")

LANES = 128
SUBLANES = 8
VMEM_LIMIT = 56 * 1024 * 1024

TM = 256
TN_IN = 1536
TN_OUT = 1024
TP = 512
QB = 256
TT = 768
TA_ROWS = 96
TZ = 384
TE = 1024

_NT = (((1,), (1,)), ((), ()))
_TN = (((0,), (0,)), ((), ()))


def _params(*sem):
    return pltpu.CompilerParams(dimension_semantics=sem, vmem_limit_bytes=VMEM_LIMIT)


def _mod_kernel(c_ref, w_ref, b_ref, o_ref):
    c = c_ref[...]
    s = (c * jax.nn.sigmoid(c)).astype(BF16)
    o_ref[...] = jnp.dot(s, w_ref[...].astype(BF16), preferred_element_type=F32) + b_ref[...]


def _modulation(c16, w_mod, b_mod):
    depth, d, n6 = w_mod.shape
    tn = 1536
    return pl.pallas_call(
        _mod_kernel,
        grid=(depth, n6 // tn),
        in_specs=[
            pl.BlockSpec((16, d), lambda l, j: (0, 0)),
            pl.BlockSpec((None, d, tn), lambda l, j: (l, 0, j)),
            pl.BlockSpec((None, 1, tn), lambda l, j: (l, 0, j)),
        ],
        out_specs=pl.BlockSpec((None, 16, tn), lambda l, j: (l, 0, j)),
        out_shape=jax.ShapeDtypeStruct((depth, 16, n6), F32),
        compiler_params=_params("parallel", "parallel"),
        name="adaln_mod",
    )(c16, w_mod, b_mod.reshape(depth, 1, n6))


def _tile_mod(modp_ref, mods_ref, comp, i, tiles_per_seq, n_ptiles, n_batch, tm):
    b = jnp.minimum(i // tiles_per_seq, n_batch - 1)
    row = modp_ref[comp, pl.ds(b, 1), :]
    srows = mods_ref[comp]
    samp = jnp.concatenate([srows, jnp.zeros((tm - srows.shape[0], srows.shape[1]), F32)], axis=0)
    return jnp.where(i >= n_ptiles, samp, row)


def _mixer_in_kernel(x_ref, modp_ref, mods_ref, nw_ref, w_ref, qn_ref, kn_ref, o_ref, *,
                     tiles_per_seq, n_ptiles, n_batch, jq, jk):
    j = pl.program_id(0)
    i = pl.program_id(1)
    x = x_ref[...]
    n = x * lax.rsqrt(jnp.mean(x * x, axis=-1, keepdims=True) + EPS) * nw_ref[...]
    sh = _tile_mod(modp_ref, mods_ref, 0, i, tiles_per_seq, n_ptiles, n_batch, TM)
    sc = _tile_mod(modp_ref, mods_ref, 1, i, tiles_per_seq, n_ptiles, n_batch, TM)
    h = (n * (1.0 + sc) + sh).astype(BF16)
    z = jnp.dot(h, w_ref[...], preferred_element_type=F32)

    def head_norm(wn_ref):
        for hh in range(N_HEADS):
            sl = slice(hh * HEAD_DIM, (hh + 1) * HEAD_DIM)
            zg = z[:, sl]
            o_ref[:, sl] = zg * lax.rsqrt(jnp.mean(zg * zg, axis=-1, keepdims=True) + EPS) * wn_ref[...]

    @pl.when(j == jq)
    def _():
        head_norm(qn_ref)

    @pl.when(j == jk)
    def _():
        head_norm(kn_ref)

    @pl.when(jnp.logical_and(j != jq, j != jk))
    def _():
        o_ref[...] = z


def _mixer_in(x, modp, mods, nw, w_bf, qn, kn, geom):
    n_pad, d = x.shape
    in_w = w_bf.shape[1]
    off_q = 2 * d + POOL_WIDTH
    assert off_q % TN_IN == 0 and ATT_WIDTH == TN_IN
    kern = functools.partial(_mixer_in_kernel, jq=off_q // TN_IN, jk=off_q // TN_IN + 1, **geom)
    return pl.pallas_call(
        kern,
        grid=(in_w // TN_IN, n_pad // TM),
        in_specs=[
            pl.BlockSpec((TM, d), lambda j, i: (i, 0)),
            pl.BlockSpec(modp.shape, lambda j, i: (0, 0, 0)),
            pl.BlockSpec(mods.shape, lambda j, i: (0, 0, 0)),
            pl.BlockSpec((1, d), lambda j, i: (0, 0)),
            pl.BlockSpec((d, TN_IN), lambda j, i: (0, j)),
            pl.BlockSpec((1, HEAD_DIM), lambda j, i: (0, 0)),
            pl.BlockSpec((1, HEAD_DIM), lambda j, i: (0, 0)),
        ],
        out_specs=pl.BlockSpec((TM, TN_IN), lambda j, i: (i, j)),
        out_shape=jax.ShapeDtypeStruct((n_pad, in_w), F32),
        compiler_params=_params("parallel", "parallel"),
        name="mixer_in",
    )(x, modp, mods, nw, w_bf, qn, kn)


def _pool_compute(ext, pos_first, n_rows, pw_ref, ps_ref):
    rowpos = pos_first + lax.broadcasted_iota(jnp.int32, (n_rows, 1), 0)
    outs = []
    for g, w in enumerate(POOL_WINDOWS):
        sl = slice(g * POOL_GROUP_DIM, (g + 1) * POOL_GROUP_DIM)
        xg = ext[:, sl]
        s = xg
        k = 1
        while k < w:
            s = s + pltpu.roll(s, k, axis=0)
            k *= 2
        cnt = jnp.minimum(rowpos + 1, w).astype(F32)
        dlt = s[POOL_HALO:] / cnt - xg[POOL_HALO:]
        y = jnp.dot(dlt.astype(BF16), pw_ref[g].astype(BF16), preferred_element_type=F32)
        outs.append(y * ps_ref[:, sl])
    return jnp.concatenate(outs, axis=-1)


def _pool_prompt_kernel(cur_ref, halo_ref, pw_ref, ps_ref, o_ref):
    t = pl.program_id(1)
    halo = jnp.where(t == 0, 0.0, halo_ref[...])
    ext = jnp.concatenate([halo, cur_ref[...]], axis=0)
    o_ref[...] = _pool_compute(ext, t * TP, TP, pw_ref, ps_ref)


def _pool_prompt(z, pool_w, pool_scale, n_batch, seq, off_u):
    tps = seq // TP
    ub = off_u // POOL_WIDTH
    return pl.pallas_call(
        _pool_prompt_kernel,
        grid=(n_batch, tps),
        in_specs=[
            pl.BlockSpec((TP, POOL_WIDTH), lambda b, t: (b * tps + t, ub)),
            pl.BlockSpec((POOL_HALO, POOL_WIDTH),
                         lambda b, t: (jnp.maximum((b * seq + t * TP) // POOL_HALO - 1, 0), ub)),
            pl.BlockSpec(pool_w.shape, lambda b, t: (0, 0, 0)),
            pl.BlockSpec((1, POOL_WIDTH), lambda b, t: (0, 0)),
        ],
        out_specs=pl.BlockSpec((TP, POOL_WIDTH), lambda b, t: (b * tps + t, 0)),
        out_shape=jax.ShapeDtypeStruct((n_batch * seq, POOL_WIDTH), F32),
        compiler_params=_params("parallel", "parallel"),
        name="pool_prompt",
    )(z, z, pool_w, pool_scale)


def _top_blocks(gate, n_valid, n_blocks):
    col = lax.broadcasted_iota(jnp.int32, gate.shape, 1)
    g = jnp.where(col < n_valid, gate, NEG_INF)
    sels = []
    for r in range(MOBA_TOPK):
        m = jnp.max(g, axis=1, keepdims=True)
        idx = jnp.min(jnp.where(g == m, col, n_blocks), axis=1, keepdims=True)
        sels.append(jnp.where(r < n_valid, idx, -1))
        g = jnp.where(col == idx, NEG_INF, g)
    return sels


def _moba_prompt_kernel(q_ref, k_ref, v_ref, bias_ref, far_ref, o_ref, kmean_ref, kb_ref, vb_ref, s_ref, *,
                        n_blocks):
    j = pl.program_id(2)
    blk = MOBA_BLOCK

    @pl.when(j == 0)
    def _():
        for n in range(n_blocks):
            kn = k_ref[n * blk:(n + 1) * blk, :]
            kmean_ref[n:n + 1, :] = jnp.sum(kn, axis=0, keepdims=True) * (1.0 / blk)
            kb_ref[n * blk:(n + 1) * blk, :] = kn.astype(BF16)
            vb_ref[n * blk:(n + 1) * blk, :] = v_ref[n * blk:(n + 1) * blk, :].astype(BF16)

    q = q_ref[...]
    gate = lax.dot_general(q, kmean_ref[...], _NT, precision=HIGHEST, preferred_element_type=F32)
    sels = _top_blocks(gate, j, n_blocks)
    qb = (q * ATT_SCALE).astype(BF16)
    far = far_ref[0:1, 0:1]

    def keys(n):
        return kb_ref[pl.ds(pl.multiple_of(n * blk, blk), blk), :]

    def values(n):
        return vb_ref[pl.ds(pl.multiple_of(n * blk, blk), blk), :]

    n_pairs = (j + 1) // 2
    row = lax.broadcasted_iota(jnp.int32, (blk, blk), 0)
    col = lax.broadcasted_iota(jnp.int32, (blk, blk), 1)
    s_own = jnp.where(col <= row, lax.dot_general(qb, keys(j), _NT, preferred_element_type=F32) + bias_ref[0],
                      NEG_INF)
    m = jnp.max(s_own, axis=1, keepdims=True)

    def pass1(p, m):
        for r in range(2):
            n = 2 * p + r
            bias = jnp.where(n == j - 1, bias_ref[1], far)
            picked = (sels[0] == n) | (sels[1] == n) | (sels[2] == n)
            s = jnp.where(picked, lax.dot_general(qb, keys(jnp.minimum(n, n_blocks - 1)), _NT,
                                                  preferred_element_type=F32) + bias, NEG_INF)
            s_ref[n] = s
            m = jnp.maximum(m, jnp.max(s, axis=1, keepdims=True))
        return m

    m = lax.fori_loop(0, n_pairs, pass1, m)
    p_own = jnp.exp(s_own - m)
    l = jnp.sum(p_own, axis=1, keepdims=True)
    acc = jnp.dot(p_own.astype(BF16), values(j), preferred_element_type=F32)

    def pass2(p, carry):
        l, acc = carry
        for r in range(2):
            n = 2 * p + r
            pr = jnp.exp(s_ref[n] - m)
            l = l + jnp.sum(pr, axis=1, keepdims=True)
            acc = acc + jnp.dot(pr.astype(BF16), values(jnp.minimum(n, n_blocks - 1)),
                                preferred_element_type=F32)
        return l, acc

    l, acc = lax.fori_loop(0, n_pairs, pass2, (l, acc))
    o_ref[...] = acc / l


def _moba_prompt(z, bias_tiles, far, n_batch, seq, off_q):
    n_blocks = seq // MOBA_BLOCK
    n_q = seq // QB
    qb0 = off_q // HEAD_DIM
    kb0 = qb0 + N_HEADS
    vb0 = kb0 + N_HEADS
    return pl.pallas_call(
        functools.partial(_moba_prompt_kernel, n_blocks=n_blocks),
        grid=(n_batch, N_HEADS, n_q),
        in_specs=[
            pl.BlockSpec((QB, HEAD_DIM), lambda b, h, j: (b * n_q + j, qb0 + h)),
            pl.BlockSpec((seq, HEAD_DIM), lambda b, h, j: (b, kb0 + h)),
            pl.BlockSpec((seq, HEAD_DIM), lambda b, h, j: (b, vb0 + h)),
            pl.BlockSpec((None, 2, MOBA_BLOCK, MOBA_BLOCK), lambda b, h, j: (h, 0, 0, 0)),
            pl.BlockSpec((None, SUBLANES, LANES), lambda b, h, j: (h, 0, 0)),
        ],
        out_specs=pl.BlockSpec((QB, HEAD_DIM), lambda b, h, j: (b * n_q + j, h)),
        out_shape=jax.ShapeDtypeStruct((n_batch * seq, ATT_WIDTH), F32),
        scratch_shapes=[pltpu.VMEM((n_blocks, HEAD_DIM), F32), pltpu.VMEM((seq, HEAD_DIM), BF16),
                        pltpu.VMEM((seq, HEAD_DIM), BF16), pltpu.VMEM((n_blocks, QB, MOBA_BLOCK), F32)],
        compiler_params=_params("parallel", "parallel", "arbitrary"),
        name="moba_prompt",
    )(z, z, z, bias_tiles, far)


def _kmean_kernel(pt_ref, c_ref, o_ref, *, pages_per_block):
    p = pl.program_id(1)
    s = jnp.sum(c_ref[...], axis=0)

    @pl.when(p % pages_per_block == 0)
    def _():
        o_ref[...] = s

    @pl.when(p % pages_per_block != 0)
    def _():
        o_ref[...] = o_ref[...] + s

    @pl.when(p % pages_per_block == pages_per_block - 1)
    def _():
        o_ref[...] = o_ref[...] * (1.0 / MOBA_BLOCK)


def _head_major(cache):
    n_pool, page, depth, heads, hd = cache.shape
    return cache.transpose(0, 1, 3, 2, 4).reshape(n_pool, page, heads * depth, hd)


def _cache_block_means(cache_k4, page_table):
    n_pool, page, rows, hd = cache_k4.shape
    db, n_pages = page_table.shape
    ppb = MOBA_BLOCK // page
    grid_spec = pltpu.PrefetchScalarGridSpec(
        num_scalar_prefetch=1,
        grid=(db, n_pages),
        in_specs=[pl.BlockSpec((None, page, rows, hd), lambda b, p, pt: (pt[b * n_pages + p], 0, 0, 0))],
        out_specs=pl.BlockSpec((None, None, rows, hd), lambda b, p, pt: (b, p // ppb, 0, 0)),
    )
    return pl.pallas_call(
        functools.partial(_kmean_kernel, pages_per_block=ppb),
        grid_spec=grid_spec,
        out_shape=jax.ShapeDtypeStruct((db, n_pages // ppb, rows, hd), F32),
        compiler_params=_params("parallel", "arbitrary"),
        name="cache_block_means",
    )(page_table.reshape(-1), cache_k4)


def _sample_pre_kernel(u_ref, st_ref, q_ref, km_ref, pw_ref, ps_ref, yp_ref, idx_ref, *,
                       past_len, n_blocks):
    ext = jnp.concatenate([st_ref[...], u_ref[...]], axis=0)
    yp_ref[...] = _pool_compute(ext, past_len, SUBLANES, pw_ref, ps_ref)
    lane = lax.broadcasted_iota(jnp.int32, (SUBLANES, LANES), 1)
    for h in range(N_HEADS):
        qh = q_ref[:, h * HEAD_DIM:(h + 1) * HEAD_DIM]
        gate = lax.dot_general(qh, km_ref[h], _NT, precision=HIGHEST, preferred_element_type=F32)
        sels = _top_blocks(gate, n_blocks, n_blocks)
        out = jnp.zeros((SUBLANES, LANES), jnp.int32)
        for r in range(MOBA_TOPK):
            out = jnp.where(lane == r, sels[r], out)
        idx_ref[h] = out


def _sample_pre(zs8, state16, kmean_l, pool_w, pool_scale, past_len, off_u, off_q):
    db = zs8.shape[0]
    n_blocks = kmean_l.shape[2]
    return pl.pallas_call(
        functools.partial(_sample_pre_kernel, past_len=past_len, n_blocks=n_blocks),
        grid=(db,),
        in_specs=[
            pl.BlockSpec((None, SUBLANES, POOL_WIDTH), lambda b: (b, 0, off_u // POOL_WIDTH)),
            pl.BlockSpec((None, POOL_HALO, POOL_WIDTH), lambda b: (b, 0, 0)),
            pl.BlockSpec((None, SUBLANES, ATT_WIDTH), lambda b: (b, 0, off_q // ATT_WIDTH)),
            pl.BlockSpec((None, N_HEADS, n_blocks, HEAD_DIM), lambda b: (b, 0, 0, 0)),
            pl.BlockSpec(pool_w.shape, lambda b: (0, 0, 0)),
            pl.BlockSpec((1, POOL_WIDTH), lambda b: (0, 0)),
        ],
        out_specs=[
            pl.BlockSpec((None, SUBLANES, POOL_WIDTH), lambda b: (b, 0, 0)),
            pl.BlockSpec((None, N_HEADS, SUBLANES, LANES), lambda b: (b, 0, 0, 0)),
        ],
        out_shape=[
            jax.ShapeDtypeStruct((db, SUBLANES, POOL_WIDTH), F32),
            jax.ShapeDtypeStruct((db, N_HEADS, SUBLANES, LANES), jnp.int32),
        ],
        compiler_params=_params("parallel"),
        name="sample_pre",
    )(zs8, state16, zs8, kmean_l, pool_w, pool_scale)


def _sample_att_kernel(pages_ref, sel_ref, q_ref, kn_ref, vn_ref, bias_ref, far_ref, ck_ref, cv_ref,
                       o_ref, kbuf, vbuf, sem, *, layer, depth, t_new, last_block, pages_per_block):
    b = pl.program_id(0)
    h = pl.program_id(1)
    base = (b * N_HEADS + h) * t_new * MOBA_TOPK
    lh = h * depth + layer

    def copies(t, r, pg):
        page = pages_ref[(base + t * MOBA_TOPK + r) * pages_per_block + pg]
        dst = pl.ds(pg * PAGE_SIZE, PAGE_SIZE)
        return (pltpu.make_async_copy(ck_ref.at[page, :, lh, :], kbuf.at[t, r, dst, :], sem.at[0]),
                pltpu.make_async_copy(cv_ref.at[page, :, lh, :], vbuf.at[t, r, dst, :], sem.at[1]))

    slots = [(t, r, pg) for t in range(t_new) for r in range(MOBA_TOPK) for pg in range(pages_per_block)]
    for s in slots:
        for c in copies(*s):
            c.start()
    for s in slots:
        for c in copies(*s):
            c.wait()

    q = q_ref[...]
    qb = q.astype(BF16)
    far = far_ref[0:1, 0:1]
    row = lax.broadcasted_iota(jnp.int32, (SUBLANES, SUBLANES), 0)
    col = lax.broadcasted_iota(jnp.int32, (SUBLANES, SUBLANES), 1)
    s_own = lax.dot_general(qb, kn_ref[...].astype(BF16), _NT, preferred_element_type=F32) * ATT_SCALE
    s_own = jnp.where(jnp.logical_and(col <= row, col < t_new), s_own + bias_ref[0, :, 0:SUBLANES], NEG_INF)
    vn = vn_ref[...].astype(BF16)
    orow = lax.broadcasted_iota(jnp.int32, (SUBLANES, HEAD_DIM), 0)
    out = jnp.zeros((SUBLANES, HEAD_DIM), F32)
    n_sel = MOBA_TOPK * MOBA_BLOCK
    for t in range(t_new):
        kt = kbuf[t].reshape(n_sel, HEAD_DIM).astype(BF16)
        vt = vbuf[t].reshape(n_sel, HEAD_DIM).astype(BF16)
        s_sel = lax.dot_general(qb, kt, _NT, preferred_element_type=F32) * ATT_SCALE
        bias = jnp.concatenate(
            [jnp.where(sel_ref[base + t * MOBA_TOPK + r] == last_block, bias_ref[1],
                       jnp.broadcast_to(far, (SUBLANES, MOBA_BLOCK))) for r in range(MOBA_TOPK)], axis=1)
        s_sel = s_sel + bias
        m = jnp.maximum(jnp.max(s_sel, axis=1, keepdims=True), jnp.max(s_own, axis=1, keepdims=True))
        p_sel = jnp.exp(s_sel - m)
        p_own = jnp.exp(s_own - m)
        l = jnp.sum(p_sel, axis=1, keepdims=True) + jnp.sum(p_own, axis=1, keepdims=True)
        o_t = (jnp.dot(p_sel.astype(BF16), vt, preferred_element_type=F32)
               + jnp.dot(p_own.astype(BF16), vn, preferred_element_type=F32)) / l
        out = jnp.where(orow == t, o_t, out)
    o_ref[...] = out


def _sample_att(pages, sel, zs8, bias_tiles, far, cache_k4, cache_v4, layer, depth, t_new, n_blocks, off_q):
    db = zs8.shape[0]
    page = cache_k4.shape[1]
    ppb = MOBA_BLOCK // page
    qb0 = off_q // HEAD_DIM
    kb0 = qb0 + N_HEADS
    vb0 = kb0 + N_HEADS
    grid_spec = pltpu.PrefetchScalarGridSpec(
        num_scalar_prefetch=2,
        grid=(db, N_HEADS),
        in_specs=[
            pl.BlockSpec((None, SUBLANES, HEAD_DIM), lambda b, h, pg, sl: (b, 0, qb0 + h)),
            pl.BlockSpec((None, SUBLANES, HEAD_DIM), lambda b, h, pg, sl: (b, 0, kb0 + h)),
            pl.BlockSpec((None, SUBLANES, HEAD_DIM), lambda b, h, pg, sl: (b, 0, vb0 + h)),
            pl.BlockSpec((None, 2, SUBLANES, MOBA_BLOCK), lambda b, h, pg, sl: (h, 0, 0, 0)),
            pl.BlockSpec((None, SUBLANES, LANES), lambda b, h, pg, sl: (h, 0, 0)),
            pl.BlockSpec(memory_space=pl.ANY),
            pl.BlockSpec(memory_space=pl.ANY),
        ],
        out_specs=pl.BlockSpec((None, SUBLANES, HEAD_DIM), lambda b, h, pg, sl: (b, 0, h)),
        scratch_shapes=[
            pltpu.VMEM((t_new, MOBA_TOPK, MOBA_BLOCK, HEAD_DIM), F32),
            pltpu.VMEM((t_new, MOBA_TOPK, MOBA_BLOCK, HEAD_DIM), F32),
            pltpu.SemaphoreType.DMA((2,)),
        ],
    )
    return pl.pallas_call(
        functools.partial(_sample_att_kernel, layer=layer, depth=depth, t_new=t_new,
                          last_block=n_blocks - 1, pages_per_block=ppb),
        grid_spec=grid_spec,
        out_shape=jax.ShapeDtypeStruct((db, SUBLANES, ATT_WIDTH), F32),
        compiler_params=_params("arbitrary", "arbitrary"),
        name="sample_att",
    )(pages, sel, zs8, zs8, zs8, bias_tiles, far, cache_k4, cache_v4)


def _merge_kernel(yp_ref, ya_ref, ga_ref, gb_ref, x_ref, modp_ref, mods_ref, wpa_ref, wpb_ref, wo_ref,
                  o_ref, m_sc, *, tiles_per_seq, n_ptiles, n_batch):
    i = pl.program_id(0)
    j = pl.program_id(1)

    @pl.when(j == 0)
    def _():
        a = jnp.dot(yp_ref[...].astype(BF16), wpa_ref[...], preferred_element_type=F32)
        c = jnp.dot(ya_ref[...].astype(BF16), wpb_ref[...], preferred_element_type=F32)
        m = jax.nn.sigmoid(ga_ref[...]) * a + jax.nn.sigmoid(gb_ref[...]) * c
        m_sc[...] = m.astype(BF16)

    g1 = _tile_mod(modp_ref, mods_ref, 0, i, tiles_per_seq, n_ptiles, n_batch, TM)
    r = jnp.dot(m_sc[...], wo_ref[...], preferred_element_type=F32)
    o_ref[...] = x_ref[...] + g1 * r


def _merge(ypool, yatt, z, x, modp, mods, wpa, wpb, wo, geom):
    n_pad, d = x.shape
    return pl.pallas_call(
        functools.partial(_merge_kernel, **geom),
        grid=(n_pad // TM, d // TN_OUT),
        in_specs=[
            pl.BlockSpec((TM, POOL_WIDTH), lambda i, j: (i, 0)),
            pl.BlockSpec((TM, ATT_WIDTH), lambda i, j: (i, 0)),
            pl.BlockSpec((TM, d), lambda i, j: (i, 0)),
            pl.BlockSpec((TM, d), lambda i, j: (i, 1)),
            pl.BlockSpec((TM, TN_OUT), lambda i, j: (i, j)),
            pl.BlockSpec((1, SUBLANES, TN_OUT), lambda i, j: (0, 0, j)),
            pl.BlockSpec((1, mods.shape[1], TN_OUT), lambda i, j: (0, 0, j)),
            pl.BlockSpec(wpa.shape, lambda i, j: (0, 0)),
            pl.BlockSpec(wpb.shape, lambda i, j: (0, 0)),
            pl.BlockSpec((d, TN_OUT), lambda i, j: (0, j)),
        ],
        out_specs=pl.BlockSpec((TM, TN_OUT), lambda i, j: (i, j)),
        out_shape=jax.ShapeDtypeStruct((n_pad, d), F32),
        scratch_shapes=[pltpu.VMEM((TM, d), BF16)],
        compiler_params=_params("parallel", "arbitrary"),
        name="mixer_merge",
    )(ypool, yatt, z, z, x, modp, mods, wpa, wpb, wo)


def _top16_rows(xs):
    t = xs[0].shape[1]
    rids = [lax.broadcasted_iota(jnp.int32, x.shape, 0) for x in xs]
    r16 = lax.broadcasted_iota(jnp.int32, (PEER_TOPK, t), 0)

    def body(k, carry):
        out = []
        for (x, vals, idxs), rid in zip(carry, rids):
            m = jnp.max(x, axis=0, keepdims=True)
            i = jnp.min(jnp.where(x == m, rid, x.shape[0]), axis=0, keepdims=True)
            out.append((jnp.where(rid == i, NEG_INF, x), jnp.where(r16 == k, m, vals),
                        jnp.where(r16 == k, i, idxs)))
        return tuple(out)

    init = tuple((x, jnp.zeros((PEER_TOPK, t), F32), jnp.zeros((PEER_TOPK, t), jnp.int32)) for x in xs)
    res = lax.fori_loop(0, PEER_TOPK, body, init)
    return [(v, i) for _, v, i in res]


_PAIR_COUNT = [PEER_TOPK // (a + 1) for a in range(PEER_TOPK)]
_PAIR_START = [sum(_PAIR_COUNT[:a]) for a in range(PEER_TOPK)]
_N_PAIRS = sum(_PAIR_COUNT)
_PAIR_ROWS = -(-_N_PAIRS // SUBLANES) * SUBLANES


def _joint_top(v1, i1, v2, i2):
    t = v1.shape[1]
    rid = lax.broadcasted_iota(jnp.int32, (_PAIR_ROWS, t), 0)
    c1 = jnp.full((_PAIR_ROWS, t), NEG_INF, F32)
    k1 = jnp.zeros((_PAIR_ROWS, t), jnp.int32)
    brow = jnp.full((_PAIR_ROWS, t), -1, jnp.int32)
    for a in range(PEER_TOPK):
        in_a = jnp.logical_and(rid >= _PAIR_START[a], rid < _PAIR_START[a] + _PAIR_COUNT[a])
        c1 = jnp.where(in_a, v1[a:a + 1, :], c1)
        k1 = jnp.where(in_a, i1[a:a + 1, :], k1)
        brow = jnp.where(in_a, rid - _PAIR_START[a], brow)
    c2 = jnp.zeros((_PAIR_ROWS, t), F32)
    k2 = jnp.zeros((_PAIR_ROWS, t), jnp.int32)
    for b in range(PEER_TOPK):
        c2 = jnp.where(brow == b, v2[b:b + 1, :], c2)
        k2 = jnp.where(brow == b, i2[b:b + 1, :], k2)
    (best, pos), = _top16_rows([c1 + c2])
    e1 = jnp.zeros_like(pos)
    e2 = jnp.zeros_like(pos)
    for r in range(_N_PAIRS):
        e1 = jnp.where(pos == r, k1[r:r + 1, :], e1)
        e2 = jnp.where(pos == r, k2[r:r + 1, :], e2)
    return best, e1, e2


def _route_kernel(x_ref, modp_ref, mods_ref, nw_ref, wq_ref, sk_ref, h_ref, eid_ref, g_ref, hq_sc, *,
                  tiles_per_seq, n_ptiles, n_batch):
    i = pl.program_id(0)
    x = x_ref[...]
    n = x * lax.rsqrt(jnp.mean(x * x, axis=-1, keepdims=True) + EPS) * nw_ref[...]
    sh = _tile_mod(modp_ref, mods_ref, 0, i, tiles_per_seq, n_ptiles, n_batch, TM)
    sc = _tile_mod(modp_ref, mods_ref, 1, i, tiles_per_seq, n_ptiles, n_batch, TM)
    hb = (n * (1.0 + sc) + sh).astype(BF16)
    h_ref[...] = hb
    hq_sc[...] = jnp.dot(hb, wq_ref[...], preferred_element_type=F32)
    half = PEER_QDIM // 2
    for part in range(TM // LANES):
        rows = slice(part * LANES, (part + 1) * LANES)
        q1 = jnp.concatenate([hq_sc[rows, hd * PEER_QDIM:hd * PEER_QDIM + half] for hd in range(PEER_HEADS)], 0)
        q2 = jnp.concatenate([hq_sc[rows, hd * PEER_QDIM + half:(hd + 1) * PEER_QDIM]
                              for hd in range(PEER_HEADS)], 0)
        s1_all = lax.dot_general(sk_ref[0], q1, _NT, precision=HIGHEST, preferred_element_type=F32)
        s2_all = lax.dot_general(sk_ref[1], q2, _NT, precision=HIGHEST, preferred_element_type=F32)
        eids, gs = [], []
        for hd in range(PEER_HEADS):
            cols = slice(hd * LANES, (hd + 1) * LANES)
            (v1, i1), (v2, i2) = _top16_rows([s1_all[:, cols], s2_all[:, cols]])
            best, e1, e2 = _joint_top(v1, i1, v2, i2)
            ex = jnp.exp(best - jnp.max(best, axis=0, keepdims=True))
            eids.append(e1 * PEER_NKEYS + e2)
            gs.append(ex / jnp.sum(ex, axis=0, keepdims=True))
        eid_ref[rows, :] = jnp.concatenate(eids, axis=0).T
        g_ref[rows, :] = jnp.concatenate(gs, axis=0).T


def _route(x, modp, mods, nw, wq, sub_keys, geom):
    n_pad, d = x.shape
    return pl.pallas_call(
        functools.partial(_route_kernel, **geom),
        grid=(n_pad // TM,),
        in_specs=[
            pl.BlockSpec((TM, d), lambda i: (i, 0)),
            pl.BlockSpec(modp.shape, lambda i: (0, 0, 0)),
            pl.BlockSpec(mods.shape, lambda i: (0, 0, 0)),
            pl.BlockSpec((1, d), lambda i: (0, 0)),
            pl.BlockSpec(wq.shape, lambda i: (0, 0)),
            pl.BlockSpec(sub_keys.shape, lambda i: (0, 0, 0)),
        ],
        out_specs=[
            pl.BlockSpec((TM, d), lambda i: (i, 0)),
            pl.BlockSpec((TM, PEER_SLOTS), lambda i: (i, 0)),
            pl.BlockSpec((TM, PEER_SLOTS), lambda i: (i, 0)),
        ],
        out_shape=[
            jax.ShapeDtypeStruct((n_pad, d), BF16),
            jax.ShapeDtypeStruct((n_pad, PEER_SLOTS), jnp.int32),
            jax.ShapeDtypeStruct((n_pad, PEER_SLOTS), F32),
        ],
        scratch_shapes=[pltpu.VMEM((TM, wq.shape[1]), F32)],
        compiler_params=_params("parallel"),
        name="peer_route",
    )(x, modp, mods, nw, wq, sub_keys)


def _gelu(a):
    return 0.5 * a * (1.0 + lax.erf(a * (1.0 / math.sqrt(2.0))))


def _expert_act_kernel(h_ref, u0_ref, unext_ref, eid_ref, g_ref, z_ref, a_sc):
    j = pl.program_id(1)
    rows_per_tile = TE // PEER_NKEYS
    cur = j % 2

    @pl.when(j == 0)
    def _():
        z_ref[...] = jnp.zeros_like(z_ref)
        a_sc[0] = lax.dot_general(h_ref[...], u0_ref[...], _NT, preferred_element_type=F32)

    a_sc[1 - cur] = lax.dot_general(h_ref[...], unext_ref[...], _NT, preferred_element_type=F32)

    for sb in range(TT // TA_ROWS):
        rows = slice(sb * TA_ROWS, (sb + 1) * TA_ROWS)
        eid = eid_ref[rows, :]
        k1 = eid // PEER_NKEYS
        k2 = eid % PEER_NKEYS
        act = z_ref[rows, :]
        for rl in range(rows_per_tile):
            picked = jnp.take_along_axis(a_sc[cur, rows, rl * PEER_NKEYS:(rl + 1) * PEER_NKEYS], k2, axis=1)
            act = jnp.where(k1 == j * rows_per_tile + rl, picked, act)
        z_ref[rows, :] = act

    @pl.when(j == pl.num_programs(1) - 1)
    def _():
        z_ref[...] = g_ref[...] * _gelu(z_ref[...])


def _expert_act(h2, u_bf, eid, g):
    n_pad, d = h2.shape
    n_exp = u_bf.shape[0]
    slot_block = pl.BlockSpec((TT, PEER_SLOTS), lambda i, j: (i, 0))
    n_te = n_exp // TE
    return pl.pallas_call(
        _expert_act_kernel,
        grid=(n_pad // TT, n_te),
        in_specs=[
            pl.BlockSpec((TT, d), lambda i, j: (i, 0)),
            pl.BlockSpec((TE, d), lambda i, j: (0, 0)),
            pl.BlockSpec((TE, d), lambda i, j: (jnp.minimum(j + 1, n_te - 1), 0)),
            slot_block, slot_block,
        ],
        out_specs=slot_block,
        out_shape=jax.ShapeDtypeStruct((n_pad, PEER_SLOTS), F32),
        scratch_shapes=[pltpu.VMEM((2, TT, TE), F32)],
        compiler_params=_params("parallel", "arbitrary"),
        name="peer_act",
    )(h2, u_bf, u_bf, eid, g)


def _expert_out_kernel(eid_ref, z_ref, v_ref, o_ref, zall_sc):
    j = pl.program_id(1)
    rows_per_tile = TE // PEER_NKEYS
    n_groups = PEER_NKEYS // rows_per_tile

    @pl.when(j == 0)
    def _():
        sub = lax.broadcasted_iota(jnp.int32, (PEER_NKEYS, PEER_SLOTS), 0)

        def token(t, carry):
            eid = eid_ref[pl.ds(t, 1), :]
            zrow = z_ref[pl.ds(t, 1), :]
            lhs = jnp.where(eid // PEER_NKEYS == sub, zrow, 0.0).astype(BF16)
            rhs = jnp.where(eid % PEER_NKEYS == sub, 1.0, 0.0).astype(BF16)
            zt = lax.dot_general(lhs, rhs, _NT, preferred_element_type=F32)
            base = pl.multiple_of(t * rows_per_tile, rows_per_tile)
            for gi in range(n_groups):
                zall_sc[gi, pl.ds(base, rows_per_tile), :] = zt[gi * rows_per_tile:(gi + 1) * rows_per_tile, :]
            return carry

        lax.fori_loop(0, TZ, token, 0, unroll=4)

    y = None
    for pair in range(rows_per_tile // 2):
        zr = jnp.concatenate(
            [zall_sc[j, pl.ds(2 * pair + r, TZ, stride=rows_per_tile), :] for r in range(2)], axis=1)
        part = jnp.dot(zr.astype(BF16), v_ref[2 * pair * PEER_NKEYS:(2 * pair + 2) * PEER_NKEYS, :],
                       preferred_element_type=F32)
        y = part if y is None else y + part

    @pl.when(j == 0)
    def _():
        o_ref[...] = y

    @pl.when(j != 0)
    def _():
        o_ref[...] = o_ref[...] + y


def _expert_out(eid, z, v_bf):
    n_pad = eid.shape[0]
    n_exp, d = v_bf.shape
    rows_per_tile = TE // PEER_NKEYS
    assert rows_per_tile == SUBLANES
    slot_block = pl.BlockSpec((TZ, PEER_SLOTS), lambda i, j: (i, 0))
    return pl.pallas_call(
        _expert_out_kernel,
        grid=(n_pad // TZ, n_exp // TE),
        in_specs=[slot_block, slot_block, pl.BlockSpec((TE, d), lambda i, j: (j, 0))],
        out_specs=pl.BlockSpec((TZ, d), lambda i, j: (i, 0)),
        out_shape=jax.ShapeDtypeStruct((n_pad, d), F32),
        scratch_shapes=[pltpu.VMEM((PEER_NKEYS // rows_per_tile, TZ * rows_per_tile, PEER_NKEYS), F32)],
        compiler_params=_params("parallel", "arbitrary"),
        name="peer_out",
    )(eid, z, v_bf)


def _residual_kernel(x_ref, y_ref, modp_ref, mods_ref, o_ref, *, tiles_per_seq, n_ptiles, n_batch):
    i = pl.program_id(0)
    g2 = _tile_mod(modp_ref, mods_ref, 0, i, tiles_per_seq, n_ptiles, n_batch, TM)
    o_ref[...] = x_ref[...] + g2 * y_ref[...]


def _residual(x, y, modp, mods, geom):
    n_pad, d = x.shape
    return pl.pallas_call(
        functools.partial(_residual_kernel, **geom),
        grid=(n_pad // TM,),
        in_specs=[
            pl.BlockSpec((TM, d), lambda i: (i, 0)),
            pl.BlockSpec((TM, d), lambda i: (i, 0)),
            pl.BlockSpec(modp.shape, lambda i: (0, 0, 0)),
            pl.BlockSpec(mods.shape, lambda i: (0, 0, 0)),
        ],
        out_specs=pl.BlockSpec((TM, d), lambda i: (i, 0)),
        out_shape=jax.ShapeDtypeStruct((n_pad, d), F32),
        compiler_params=_params("parallel"),
        name="gated_residual",
    )(x, y, modp, mods)


def _rel_bucket(dist):
    n = jnp.maximum(dist, 0)
    max_exact = N_BUCKETS // 2
    nf = jnp.maximum(n, 1).astype(F32)
    large = max_exact + (jnp.log(nf / max_exact) / math.log(MAX_DISTANCE / max_exact)
                         * (N_BUCKETS - max_exact)).astype(jnp.int32)
    return jnp.where(n < max_exact, n, jnp.minimum(large, N_BUCKETS - 1))


def _bias_tiles(rel_bias):
    assert MOBA_BLOCK + 1 >= MAX_DISTANCE
    blk = MOBA_BLOCK

    def toeplitz(dist):
        u = rel_bias[_rel_bucket(dist)].T
        flat = jnp.tile(u, (1, blk + 1))[:, :blk * (2 * blk + 1)]
        return flat.reshape(N_HEADS, blk, 2 * blk + 1)[:, :, :blk][:, :, ::-1]

    k = jnp.arange(2 * blk)
    own = toeplitz(jnp.maximum(k - (blk - 1), 0))
    prev = toeplitz(jnp.minimum(k + 1, 2 * blk - 1))
    tiles = jnp.stack([own, prev], axis=1)
    far = jnp.broadcast_to(rel_bias[N_BUCKETS - 1][:, None, None], (N_HEADS, SUBLANES, LANES))
    return tiles, far


def kernel(x_prompt, x_sample, c_prompt, c_sample, cache_k, cache_v, state_pool, page_table, rel_bias,
           w_mod, b_mod, norm1_w, w_in, q_norm_w, k_norm_w, pool_w, pool_scale, w_pa, w_pb, w_out,
           norm2_w, peer_wq, peer_sub_keys, peer_u, peer_v):
    n_batch, seq, d = x_prompt.shape
    db, t_new, _ = x_sample.shape
    depth = w_mod.shape[0]
    n_pages = page_table.shape[1]
    past_len = n_pages * PAGE_SIZE
    n_prompt = n_batch * seq
    n_samp = db * t_new
    n_past_blocks = past_len // MOBA_BLOCK
    assert past_len % MOBA_BLOCK == 0 and n_past_blocks >= MOBA_TOPK
    assert seq % TP == 0 and seq % MOBA_BLOCK == 0 and n_samp <= TM and n_samp % SUBLANES == 0
    assert PEER_SLOTS == PEER_NKEYS and MOBA_BLOCK == QB and TM % LANES == 0
    assert t_new <= SUBLANES and n_batch <= SUBLANES and n_batch + db <= 16
    assert n_prompt % TM == 0
    step = math.lcm(TM, TT, TZ)
    n_pad = -(-(n_prompt + n_samp) // step) * step
    geom = dict(tiles_per_seq=seq // TM, n_ptiles=n_prompt // TM, n_batch=n_batch)

    off_ga, off_gb, off_u = 0, d, 2 * d
    off_q = off_u + POOL_WIDTH
    off_k = off_q + ATT_WIDTH
    off_v = off_k + ATT_WIDTH
    o_q = POOL_WIDTH
    o_ga = o_q + 3 * ATT_WIDTH
    w_in_bf = jnp.concatenate([w_in[:, :, o_ga:], w_in[:, :, :o_ga]], axis=-1).astype(BF16)
    w_pa_bf, w_pb_bf, w_out_bf = w_pa.astype(BF16), w_pb.astype(BF16), w_out.astype(BF16)
    wq_bf, u_bf, v_bf = peer_wq.astype(BF16), peer_u.astype(BF16), peer_v.astype(BF16)

    c16 = jnp.zeros((16, d), F32).at[:n_batch].set(c_prompt).at[n_batch:n_batch + db].set(c_sample)
    mod = _modulation(c16, w_mod, b_mod).reshape(depth, 16, 6, d).transpose(0, 2, 1, 3)
    modp_all = mod[:, :, :SUBLANES]
    mods_all = jnp.repeat(mod[:, :, n_batch:n_batch + db], t_new, axis=2)

    bias_tiles, far = _bias_tiles(rel_bias)
    cache_k4, cache_v4 = _head_major(cache_k), _head_major(cache_v)
    kmean = _cache_block_means(cache_k4, page_table).reshape(db, n_past_blocks, N_HEADS, depth, HEAD_DIM)

    x = jnp.zeros((n_pad, d), F32).at[:n_prompt].set(x_prompt.reshape(n_prompt, d))
    x = x.at[n_prompt:n_prompt + n_samp].set(x_sample.reshape(n_samp, d))

    k_rows, v_rows, pool_p, pool_s = [], [], [], []
    for l in range(depth):
        z = _mixer_in(x, modp_all[l, 0:2], mods_all[l, 0:2], norm1_w[l][None], w_in_bf[l],
                      q_norm_w[l][None], k_norm_w[l][None], geom)
        k_rows.append(z[:, off_k:off_k + ATT_WIDTH])
        v_rows.append(z[:, off_v:off_v + ATT_WIDTH])
        u_p = z[:n_prompt, off_u:off_u + POOL_WIDTH].reshape(n_batch, seq, POOL_WIDTH)
        pool_p.append(u_p[:, seq - POOL_STATE:])
        zs = z[n_prompt:n_prompt + n_samp].reshape(db, t_new, -1)
        u_s = zs[:, :, off_u:off_u + POOL_WIDTH]
        pool_s.append(jnp.concatenate([state_pool[l], u_s], axis=1)[:, -POOL_STATE:])

        ypool_p = _pool_prompt(z, pool_w[l], pool_scale[l][None], n_batch, seq, off_u)
        yatt_p = _moba_prompt(z, bias_tiles, far, n_batch, seq, off_q)

        zs8 = jnp.pad(zs, ((0, 0), (0, SUBLANES - t_new), (0, 0)))
        state16 = jnp.pad(state_pool[l], ((0, 0), (POOL_HALO - POOL_STATE, 0), (0, 0)))
        kmean_l = kmean[:, :, :, l].transpose(0, 2, 1, 3)
        ypool_s, idx = _sample_pre(zs8, state16, kmean_l, pool_w[l], pool_scale[l][None], past_len,
                                   off_u, off_q)
        sel = idx[:, :, :t_new, :MOBA_TOPK]
        ppb = MOBA_BLOCK // PAGE_SIZE
        logical = sel[..., None] * ppb + jnp.arange(ppb)
        pages = page_table[jnp.arange(db)[:, None, None, None, None], logical]
        yatt_s = _sample_att(pages.reshape(-1), sel.reshape(-1), zs8, bias_tiles, far, cache_k4, cache_v4,
                             l, depth, t_new, n_past_blocks, off_q)

        tail = n_pad - n_prompt - n_samp
        ypool = jnp.concatenate([ypool_p, ypool_s[:, :t_new].reshape(n_samp, POOL_WIDTH),
                                 jnp.zeros((tail, POOL_WIDTH), F32)], axis=0)
        yatt = jnp.concatenate([yatt_p, yatt_s[:, :t_new].reshape(n_samp, ATT_WIDTH),
                                jnp.zeros((tail, ATT_WIDTH), F32)], axis=0)
        x1 = _merge(ypool, yatt, z, x, modp_all[l, 2:3], mods_all[l, 2:3], w_pa_bf[l], w_pb_bf[l],
                    w_out_bf[l], geom)
        h2, eid, g = _route(x1, modp_all[l, 3:5], mods_all[l, 3:5], norm2_w[l][None], wq_bf[l],
                            peer_sub_keys[l], geom)
        y = _expert_out(eid, _expert_act(h2, u_bf[l], eid, g), v_bf[l])
        x = _residual(x1, y, modp_all[l, 5:6], mods_all[l, 5:6], geom)

    def rows(parts, lo, hi, lead):
        return jnp.stack([p[lo:hi] for p in parts], axis=1).reshape(*lead, depth, N_HEADS, HEAD_DIM)

    y_prompt = x[:n_prompt].reshape(n_batch, seq, d)
    y_sample = x[n_prompt:n_prompt + n_samp].reshape(db, t_new, d)
    return (y_prompt, y_sample,
            rows(k_rows, 0, n_prompt, (n_batch, seq)), rows(v_rows, 0, n_prompt, (n_batch, seq)),
            jnp.stack(pool_p, axis=0),
            rows(k_rows, n_prompt, n_prompt + n_samp, (db, t_new)),
            rows(v_rows, n_prompt, n_prompt + n_samp, (db, t_new)),
            jnp.stack(pool_s, axis=0))
```

```python
import functools
import math

import jax
import jax.numpy as jnp
from jax import lax
from jax.experimental import pallas as pl
from jax.experimental.pallas import tpu as pltpu

F32 = jnp.float32
BF16 = jnp.bfloat16
HIGHEST = lax.Precision.HIGHEST

N_HEADS = 12
HEAD_DIM = 128
ATT_WIDTH = N_HEADS * HEAD_DIM
MOBA_BLOCK = 256
MOBA_TOPK = 3
ATT_SCALE = HEAD_DIM ** -0.5
N_BUCKETS = 32
MAX_DISTANCE = 128
POOL_WINDOWS = (2, 4, 8, 16)
POOL_GROUP_DIM = 128
POOL_WIDTH = len(POOL_WINDOWS) * POOL_GROUP_DIM
POOL_STATE = max(POOL_WINDOWS) - 1
POOL_HALO = 16
PEER_HEADS = 8
PEER_QDIM = 256
PEER_NKEYS = 128
PEER_TOPK = 16
PEER_SLOTS = PEER_HEADS * PEER_TOPK
N_EXPERTS = PEER_NKEYS * PEER_NKEYS
PAGE_SIZE = 128
EPS = 1e-6
NEG_INF = float("-inf")

LANES = 128
SUBLANES = 8
VMEM_LIMIT = 56 * 1024 * 1024

TM = 256
TN_IN = 1536
TN_OUT = 1024
TP = 512
QB = 256
TT = 768
TA_ROWS = 96
TZ = 384
TE = 1024

_NT = (((1,), (1,)), ((), ()))
_TN = (((0,), (0,)), ((), ()))


def _params(*sem):
    return pltpu.CompilerParams(dimension_semantics=sem, vmem_limit_bytes=VMEM_LIMIT)


def _mod_kernel(c_ref, w_ref, b_ref, o_ref):
    c = c_ref[...]
    s = (c * jax.nn.sigmoid(c)).astype(BF16)
    o_ref[...] = jnp.dot(s, w_ref[...].astype(BF16), preferred_element_type=F32) + b_ref[...]


def _modulation(c16, w_mod, b_mod):
    depth, d, n6 = w_mod.shape
    tn = 1536
    return pl.pallas_call(
        _mod_kernel,
        grid=(depth, n6 // tn),
        in_specs=[
            pl.BlockSpec((16, d), lambda l, j: (0, 0)),
            pl.BlockSpec((None, d, tn), lambda l, j: (l, 0, j)),
            pl.BlockSpec((None, 1, tn), lambda l, j: (l, 0, j)),
        ],
        out_specs=pl.BlockSpec((None, 16, tn), lambda l, j: (l, 0, j)),
        out_shape=jax.ShapeDtypeStruct((depth, 16, n6), F32),
        compiler_params=_params("parallel", "parallel"),
        name="adaln_mod",
    )(c16, w_mod, b_mod.reshape(depth, 1, n6))


def _tile_mod(modp_ref, mods_ref, comp, i, tiles_per_seq, n_ptiles, n_batch, tm):
    b = jnp.minimum(i // tiles_per_seq, n_batch - 1)
    row = modp_ref[comp, pl.ds(b, 1), :]
    srows = mods_ref[comp]
    samp = jnp.concatenate([srows, jnp.zeros((tm - srows.shape[0], srows.shape[1]), F32)], axis=0)
    return jnp.where(i >= n_ptiles, samp, row)


def _mixer_in_kernel(x_ref, modp_ref, mods_ref, nw_ref, w_ref, qn_ref, kn_ref, o_ref, *,
                     tiles_per_seq, n_ptiles, n_batch, jq, jk):
    j = pl.program_id(0)
    i = pl.program_id(1)
    x = x_ref[...]
    n = x * lax.rsqrt(jnp.mean(x * x, axis=-1, keepdims=True) + EPS) * nw_ref[...]
    sh = _tile_mod(modp_ref, mods_ref, 0, i, tiles_per_seq, n_ptiles, n_batch, TM)
    sc = _tile_mod(modp_ref, mods_ref, 1, i, tiles_per_seq, n_ptiles, n_batch, TM)
    h = (n * (1.0 + sc) + sh).astype(BF16)
    z = jnp.dot(h, w_ref[...], preferred_element_type=F32)

    def head_norm(wn_ref):
        for hh in range(N_HEADS):
            sl = slice(hh * HEAD_DIM, (hh + 1) * HEAD_DIM)
            zg = z[:, sl]
            o_ref[:, sl] = zg * lax.rsqrt(jnp.mean(zg * zg, axis=-1, keepdims=True) + EPS) * wn_ref[...]

    @pl.when(j == jq)
    def _():
        head_norm(qn_ref)

    @pl.when(j == jk)
    def _():
        head_norm(kn_ref)

    @pl.when(jnp.logical_and(j != jq, j != jk))
    def _():
        o_ref[...] = z


def _mixer_in(x, modp, mods, nw, w_bf, qn, kn, geom):
    n_pad, d = x.shape
    in_w = w_bf.shape[1]
    off_q = 2 * d + POOL_WIDTH
    assert off_q % TN_IN == 0 and ATT_WIDTH == TN_IN
    kern = functools.partial(_mixer_in_kernel, jq=off_q // TN_IN, jk=off_q // TN_IN + 1, **geom)
    return pl.pallas_call(
        kern,
        grid=(in_w // TN_IN, n_pad // TM),
        in_specs=[
            pl.BlockSpec((TM, d), lambda j, i: (i, 0)),
            pl.BlockSpec(modp.shape, lambda j, i: (0, 0, 0)),
            pl.BlockSpec(mods.shape, lambda j, i: (0, 0, 0)),
            pl.BlockSpec((1, d), lambda j, i: (0, 0)),
            pl.BlockSpec((d, TN_IN), lambda j, i: (0, j)),
            pl.BlockSpec((1, HEAD_DIM), lambda j, i: (0, 0)),
            pl.BlockSpec((1, HEAD_DIM), lambda j, i: (0, 0)),
        ],
        out_specs=pl.BlockSpec((TM, TN_IN), lambda j, i: (i, j)),
        out_shape=jax.ShapeDtypeStruct((n_pad, in_w), F32),
        compiler_params=_params("parallel", "parallel"),
        name="mixer_in",
    )(x, modp, mods, nw, w_bf, qn, kn)


def _pool_compute(ext, pos_first, n_rows, pw_ref, ps_ref):
    rowpos = pos_first + lax.broadcasted_iota(jnp.int32, (n_rows, 1), 0)
    outs = []
    for g, w in enumerate(POOL_WINDOWS):
        sl = slice(g * POOL_GROUP_DIM, (g + 1) * POOL_GROUP_DIM)
        xg = ext[:, sl]
        s = xg
        k = 1
        while k < w:
            s = s + pltpu.roll(s, k, axis=0)
            k *= 2
        cnt = jnp.minimum(rowpos + 1, w).astype(F32)
        dlt = s[POOL_HALO:] / cnt - xg[POOL_HALO:]
        y = jnp.dot(dlt.astype(BF16), pw_ref[g].astype(BF16), preferred_element_type=F32)
        outs.append(y * ps_ref[:, sl])
    return jnp.concatenate(outs, axis=-1)


def _pool_prompt_kernel(cur_ref, halo_ref, pw_ref, ps_ref, o_ref):
    t = pl.program_id(1)
    halo = jnp.where(t == 0, 0.0, halo_ref[...])
    ext = jnp.concatenate([halo, cur_ref[...]], axis=0)
    o_ref[...] = _pool_compute(ext, t * TP, TP, pw_ref, ps_ref)


def _pool_prompt(z, pool_w, pool_scale, n_batch, seq, off_u):
    tps = seq // TP
    ub = off_u // POOL_WIDTH
    return pl.pallas_call(
        _pool_prompt_kernel,
        grid=(n_batch, tps),
        in_specs=[
            pl.BlockSpec((TP, POOL_WIDTH), lambda b, t: (b * tps + t, ub)),
            pl.BlockSpec((POOL_HALO, POOL_WIDTH),
                         lambda b, t: (jnp.maximum((b * seq + t * TP) // POOL_HALO - 1, 0), ub)),
            pl.BlockSpec(pool_w.shape, lambda b, t: (0, 0, 0)),
            pl.BlockSpec((1, POOL_WIDTH), lambda b, t: (0, 0)),
        ],
        out_specs=pl.BlockSpec((TP, POOL_WIDTH), lambda b, t: (b * tps + t, 0)),
        out_shape=jax.ShapeDtypeStruct((n_batch * seq, POOL_WIDTH), F32),
        compiler_params=_params("parallel", "parallel"),
        name="pool_prompt",
    )(z, z, pool_w, pool_scale)


def _top_blocks(gate, n_valid, n_blocks):
    col = lax.broadcasted_iota(jnp.int32, gate.shape, 1)
    g = jnp.where(col < n_valid, gate, NEG_INF)
    sels = []
    for r in range(MOBA_TOPK):
        m = jnp.max(g, axis=1, keepdims=True)
        idx = jnp.min(jnp.where(g == m, col, n_blocks), axis=1, keepdims=True)
        sels.append(jnp.where(r < n_valid, idx, -1))
        g = jnp.where(col == idx, NEG_INF, g)
    return sels


def _moba_prompt_kernel(q_ref, k_ref, v_ref, bias_ref, far_ref, o_ref, kmean_ref, kb_ref, vb_ref, s_ref, *,
                        n_blocks):
    j = pl.program_id(2)
    blk = MOBA_BLOCK

    @pl.when(j == 0)
    def _():
        for n in range(n_blocks):
            kn = k_ref[n * blk:(n + 1) * blk, :]
            kmean_ref[n:n + 1, :] = jnp.sum(kn, axis=0, keepdims=True) * (1.0 / blk)
            kb_ref[n * blk:(n + 1) * blk, :] = kn.astype(BF16)
            vb_ref[n * blk:(n + 1) * blk, :] = v_ref[n * blk:(n + 1) * blk, :].astype(BF16)

    q = q_ref[...]
    gate = lax.dot_general(q, kmean_ref[...], _NT, precision=HIGHEST, preferred_element_type=F32)
    sels = _top_blocks(gate, j, n_blocks)
    qb = (q * ATT_SCALE).astype(BF16)
    far = far_ref[0:1, 0:1]

    def keys(n):
        return kb_ref[pl.ds(pl.multiple_of(n * blk, blk), blk), :]

    def values(n):
        return vb_ref[pl.ds(pl.multiple_of(n * blk, blk), blk), :]

    n_pairs = (j + 1) // 2
    row = lax.broadcasted_iota(jnp.int32, (blk, blk), 0)
    col = lax.broadcasted_iota(jnp.int32, (blk, blk), 1)
    s_own = jnp.where(col <= row, lax.dot_general(qb, keys(j), _NT, preferred_element_type=F32) + bias_ref[0],
                      NEG_INF)
    m = jnp.max(s_own, axis=1, keepdims=True)

    def pass1(p, m):
        for r in range(2):
            n = 2 * p + r
            bias = jnp.where(n == j - 1, bias_ref[1], far)
            picked = (sels[0] == n) | (sels[1] == n) | (sels[2] == n)
            s = jnp.where(picked, lax.dot_general(qb, keys(jnp.minimum(n, n_blocks - 1)), _NT,
                                                  preferred_element_type=F32) + bias, NEG_INF)
            s_ref[n] = s
            m = jnp.maximum(m, jnp.max(s, axis=1, keepdims=True))
        return m

    m = lax.fori_loop(0, n_pairs, pass1, m)
    p_own = jnp.exp(s_own - m)
    l = jnp.sum(p_own, axis=1, keepdims=True)
    acc = jnp.dot(p_own.astype(BF16), values(j), preferred_element_type=F32)

    def pass2(p, carry):
        l, acc = carry
        for r in range(2):
            n = 2 * p + r
            pr = jnp.exp(s_ref[n] - m)
            l = l + jnp.sum(pr, axis=1, keepdims=True)
            acc = acc + jnp.dot(pr.astype(BF16), values(jnp.minimum(n, n_blocks - 1)),
                                preferred_element_type=F32)
        return l, acc

    l, acc = lax.fori_loop(0, n_pairs, pass2, (l, acc))
    o_ref[...] = acc / l


def _moba_prompt(z, bias_tiles, far, n_batch, seq, off_q):
    n_blocks = seq // MOBA_BLOCK
    n_q = seq // QB
    qb0 = off_q // HEAD_DIM
    kb0 = qb0 + N_HEADS
    vb0 = kb0 + N_HEADS
    return pl.pallas_call(
        functools.partial(_moba_prompt_kernel, n_blocks=n_blocks),
        grid=(n_batch, N_HEADS, n_q),
        in_specs=[
            pl.BlockSpec((QB, HEAD_DIM), lambda b, h, j: (b * n_q + j, qb0 + h)),
            pl.BlockSpec((seq, HEAD_DIM), lambda b, h, j: (b, kb0 + h)),
            pl.BlockSpec((seq, HEAD_DIM), lambda b, h, j: (b, vb0 + h)),
            pl.BlockSpec((None, 2, MOBA_BLOCK, MOBA_BLOCK), lambda b, h, j: (h, 0, 0, 0)),
            pl.BlockSpec((None, SUBLANES, LANES), lambda b, h, j: (h, 0, 0)),
        ],
        out_specs=pl.BlockSpec((QB, HEAD_DIM), lambda b, h, j: (b * n_q + j, h)),
        out_shape=jax.ShapeDtypeStruct((n_batch * seq, ATT_WIDTH), F32),
        scratch_shapes=[pltpu.VMEM((n_blocks, HEAD_DIM), F32), pltpu.VMEM((seq, HEAD_DIM), BF16),
                        pltpu.VMEM((seq, HEAD_DIM), BF16), pltpu.VMEM((n_blocks, QB, MOBA_BLOCK), F32)],
        compiler_params=_params("parallel", "parallel", "arbitrary"),
        name="moba_prompt",
    )(z, z, z, bias_tiles, far)


def _kmean_kernel(pt_ref, c_ref, o_ref, *, pages_per_block):
    p = pl.program_id(1)
    s = jnp.sum(c_ref[...], axis=0)

    @pl.when(p % pages_per_block == 0)
    def _():
        o_ref[...] = s

    @pl.when(p % pages_per_block != 0)
    def _():
        o_ref[...] = o_ref[...] + s

    @pl.when(p % pages_per_block == pages_per_block - 1)
    def _():
        o_ref[...] = o_ref[...] * (1.0 / MOBA_BLOCK)


def _head_major(cache):
    n_pool, page, depth, heads, hd = cache.shape
    return cache.transpose(0, 1, 3, 2, 4).reshape(n_pool, page, heads * depth, hd)


def _cache_block_means(cache_k4, page_table):
    n_pool, page, rows, hd = cache_k4.shape
    db, n_pages = page_table.shape
    ppb = MOBA_BLOCK // page
    grid_spec = pltpu.PrefetchScalarGridSpec(
        num_scalar_prefetch=1,
        grid=(db, n_pages),
        in_specs=[pl.BlockSpec((None, page, rows, hd), lambda b, p, pt: (pt[b * n_pages + p], 0, 0, 0))],
        out_specs=pl.BlockSpec((None, None, rows, hd), lambda b, p, pt: (b, p // ppb, 0, 0)),
    )
    return pl.pallas_call(
        functools.partial(_kmean_kernel, pages_per_block=ppb),
        grid_spec=grid_spec,
        out_shape=jax.ShapeDtypeStruct((db, n_pages // ppb, rows, hd), F32),
        compiler_params=_params("parallel", "arbitrary"),
        name="cache_block_means",
    )(page_table.reshape(-1), cache_k4)


def _sample_pre_kernel(u_ref, st_ref, q_ref, km_ref, pw_ref, ps_ref, yp_ref, idx_ref, *,
                       past_len, n_blocks):
    ext = jnp.concatenate([st_ref[...], u_ref[...]], axis=0)
    yp_ref[...] = _pool_compute(ext, past_len, SUBLANES, pw_ref, ps_ref)
    lane = lax.broadcasted_iota(jnp.int32, (SUBLANES, LANES), 1)
    for h in range(N_HEADS):
        qh = q_ref[:, h * HEAD_DIM:(h + 1) * HEAD_DIM]
        gate = lax.dot_general(qh, km_ref[h], _NT, precision=HIGHEST, preferred_element_type=F32)
        sels = _top_blocks(gate, n_blocks, n_blocks)
        out = jnp.zeros((SUBLANES, LANES), jnp.int32)
        for r in range(MOBA_TOPK):
            out = jnp.where(lane == r, sels[r], out)
        idx_ref[h] = out


def _sample_pre(zs8, state16, kmean_l, pool_w, pool_scale, past_len, off_u, off_q):
    db = zs8.shape[0]
    n_blocks = kmean_l.shape[2]
    return pl.pallas_call(
        functools.partial(_sample_pre_kernel, past_len=past_len, n_blocks=n_blocks),
        grid=(db,),
        in_specs=[
            pl.BlockSpec((None, SUBLANES, POOL_WIDTH), lambda b: (b, 0, off_u // POOL_WIDTH)),
            pl.BlockSpec((None, POOL_HALO, POOL_WIDTH), lambda b: (b, 0, 0)),
            pl.BlockSpec((None, SUBLANES, ATT_WIDTH), lambda b: (b, 0, off_q // ATT_WIDTH)),
            pl.BlockSpec((None, N_HEADS, n_blocks, HEAD_DIM), lambda b: (b, 0, 0, 0)),
            pl.BlockSpec(pool_w.shape, lambda b: (0, 0, 0)),
            pl.BlockSpec((1, POOL_WIDTH), lambda b: (0, 0)),
        ],
        out_specs=[
            pl.BlockSpec((None, SUBLANES, POOL_WIDTH), lambda b: (b, 0, 0)),
            pl.BlockSpec((None, N_HEADS, SUBLANES, LANES), lambda b: (b, 0, 0, 0)),
        ],
        out_shape=[
            jax.ShapeDtypeStruct((db, SUBLANES, POOL_WIDTH), F32),
            jax.ShapeDtypeStruct((db, N_HEADS, SUBLANES, LANES), jnp.int32),
        ],
        compiler_params=_params("parallel"),
        name="sample_pre",
    )(zs8, state16, zs8, kmean_l, pool_w, pool_scale)


def _sample_att_kernel(pages_ref, sel_ref, q_ref, kn_ref, vn_ref, bias_ref, far_ref, ck_ref, cv_ref,
                       o_ref, kbuf, vbuf, sem, *, layer, depth, t_new, last_block, pages_per_block):
    b = pl.program_id(0)
    h = pl.program_id(1)
    base = (b * N_HEADS + h) * t_new * MOBA_TOPK
    lh = h * depth + layer

    def copies(t, r, pg):
        page = pages_ref[(base + t * MOBA_TOPK + r) * pages_per_block + pg]
        dst = pl.ds(pg * PAGE_SIZE, PAGE_SIZE)
        return (pltpu.make_async_copy(ck_ref.at[page, :, lh, :], kbuf.at[t, r, dst, :], sem.at[0]),
                pltpu.make_async_copy(cv_ref.at[page, :, lh, :], vbuf.at[t, r, dst, :], sem.at[1]))

    slots = [(t, r, pg) for t in range(t_new) for r in range(MOBA_TOPK) for pg in range(pages_per_block)]
    for s in slots:
        for c in copies(*s):
            c.start()
    for s in slots:
        for c in copies(*s):
            c.wait()

    q = q_ref[...]
    qb = q.astype(BF16)
    far = far_ref[0:1, 0:1]
    row = lax.broadcasted_iota(jnp.int32, (SUBLANES, SUBLANES), 0)
    col = lax.broadcasted_iota(jnp.int32, (SUBLANES, SUBLANES), 1)
    s_own = lax.dot_general(qb, kn_ref[...].astype(BF16), _NT, preferred_element_type=F32) * ATT_SCALE
    s_own = jnp.where(jnp.logical_and(col <= row, col < t_new), s_own + bias_ref[0, :, 0:SUBLANES], NEG_INF)
    vn = vn_ref[...].astype(BF16)
    orow = lax.broadcasted_iota(jnp.int32, (SUBLANES, HEAD_DIM), 0)
    out = jnp.zeros((SUBLANES, HEAD_DIM), F32)
    n_sel = MOBA_TOPK * MOBA_BLOCK
    for t in range(t_new):
        kt = kbuf[t].reshape(n_sel, HEAD_DIM).astype(BF16)
        vt = vbuf[t].reshape(n_sel, HEAD_DIM).astype(BF16)
        s_sel = lax.dot_general(qb, kt, _NT, preferred_element_type=F32) * ATT_SCALE
        bias = jnp.concatenate(
            [jnp.where(sel_ref[base + t * MOBA_TOPK + r] == last_block, bias_ref[1],
                       jnp.broadcast_to(far, (SUBLANES, MOBA_BLOCK))) for r in range(MOBA_TOPK)], axis=1)
        s_sel = s_sel + bias
        m = jnp.maximum(jnp.max(s_sel, axis=1, keepdims=True), jnp.max(s_own, axis=1, keepdims=True))
        p_sel = jnp.exp(s_sel - m)
        p_own = jnp.exp(s_own - m)
        l = jnp.sum(p_sel, axis=1, keepdims=True) + jnp.sum(p_own, axis=1, keepdims=True)
        o_t = (jnp.dot(p_sel.astype(BF16), vt, preferred_element_type=F32)
               + jnp.dot(p_own.astype(BF16), vn, preferred_element_type=F32)) / l
        out = jnp.where(orow == t, o_t, out)
    o_ref[...] = out


def _sample_att(pages, sel, zs8, bias_tiles, far, cache_k4, cache_v4, layer, depth, t_new, n_blocks, off_q):
    db = zs8.shape[0]
    page = cache_k4.shape[1]
    ppb = MOBA_BLOCK // page
    qb0 = off_q // HEAD_DIM
    kb0 = qb0 + N_HEADS
    vb0 = kb0 + N_HEADS
    grid_spec = pltpu.PrefetchScalarGridSpec(
        num_scalar_prefetch=2,
        grid=(db, N_HEADS),
        in_specs=[
            pl.BlockSpec((None, SUBLANES, HEAD_DIM), lambda b, h, pg, sl: (b, 0, qb0 + h)),
            pl.BlockSpec((None, SUBLANES, HEAD_DIM), lambda b, h, pg, sl: (b, 0, kb0 + h)),
            pl.BlockSpec((None, SUBLANES, HEAD_DIM), lambda b, h, pg, sl: (b, 0, vb0 + h)),
            pl.BlockSpec((None, 2, SUBLANES, MOBA_BLOCK), lambda b, h, pg, sl: (h, 0, 0, 0)),
            pl.BlockSpec((None, SUBLANES, LANES), lambda b, h, pg, sl: (h, 0, 0)),
            pl.BlockSpec(memory_space=pl.ANY),
            pl.BlockSpec(memory_space=pl.ANY),
        ],
        out_specs=pl.BlockSpec((None, SUBLANES, HEAD_DIM), lambda b, h, pg, sl: (b, 0, h)),
        scratch_shapes=[
            pltpu.VMEM((t_new, MOBA_TOPK, MOBA_BLOCK, HEAD_DIM), F32),
            pltpu.VMEM((t_new, MOBA_TOPK, MOBA_BLOCK, HEAD_DIM), F32),
            pltpu.SemaphoreType.DMA((2,)),
        ],
    )
    return pl.pallas_call(
        functools.partial(_sample_att_kernel, layer=layer, depth=depth, t_new=t_new,
                          last_block=n_blocks - 1, pages_per_block=ppb),
        grid_spec=grid_spec,
        out_shape=jax.ShapeDtypeStruct((db, SUBLANES, ATT_WIDTH), F32),
        compiler_params=_params("arbitrary", "arbitrary"),
        name="sample_att",
    )(pages, sel, zs8, zs8, zs8, bias_tiles, far, cache_k4, cache_v4)


def _merge_kernel(yp_ref, ya_ref, ga_ref, gb_ref, x_ref, modp_ref, mods_ref, wpa_ref, wpb_ref, wo_ref,
                  o_ref, m_sc, *, tiles_per_seq, n_ptiles, n_batch):
    i = pl.program_id(0)
    j = pl.program_id(1)

    @pl.when(j == 0)
    def _():
        a = jnp.dot(yp_ref[...].astype(BF16), wpa_ref[...], preferred_element_type=F32)
        c = jnp.dot(ya_ref[...].astype(BF16), wpb_ref[...], preferred_element_type=F32)
        m = jax.nn.sigmoid(ga_ref[...]) * a + jax.nn.sigmoid(gb_ref[...]) * c
        m_sc[...] = m.astype(BF16)

    g1 = _tile_mod(modp_ref, mods_ref, 0, i, tiles_per_seq, n_ptiles, n_batch, TM)
    r = jnp.dot(m_sc[...], wo_ref[...], preferred_element_type=F32)
    o_ref[...] = x_ref[...] + g1 * r


def _merge(ypool, yatt, z, x, modp, mods, wpa, wpb, wo, geom):
    n_pad, d = x.shape
    return pl.pallas_call(
        functools.partial(_merge_kernel, **geom),
        grid=(n_pad // TM, d // TN_OUT),
        in_specs=[
            pl.BlockSpec((TM, POOL_WIDTH), lambda i, j: (i, 0)),
            pl.BlockSpec((TM, ATT_WIDTH), lambda i, j: (i, 0)),
            pl.BlockSpec((TM, d), lambda i, j: (i, 0)),
            pl.BlockSpec((TM, d), lambda i, j: (i, 1)),
            pl.BlockSpec((TM, TN_OUT), lambda i, j: (i, j)),
            pl.BlockSpec((1, SUBLANES, TN_OUT), lambda i, j: (0, 0, j)),
            pl.BlockSpec((1, mods.shape[1], TN_OUT), lambda i, j: (0, 0, j)),
            pl.BlockSpec(wpa.shape, lambda i, j: (0, 0)),
            pl.BlockSpec(wpb.shape, lambda i, j: (0, 0)),
            pl.BlockSpec((d, TN_OUT), lambda i, j: (0, j)),
        ],
        out_specs=pl.BlockSpec((TM, TN_OUT), lambda i, j: (i, j)),
        out_shape=jax.ShapeDtypeStruct((n_pad, d), F32),
        scratch_shapes=[pltpu.VMEM((TM, d), BF16)],
        compiler_params=_params("parallel", "arbitrary"),
        name="mixer_merge",
    )(ypool, yatt, z, z, x, modp, mods, wpa, wpb, wo)


def _top16_rows(xs):
    t = xs[0].shape[1]
    rids = [lax.broadcasted_iota(jnp.int32, x.shape, 0) for x in xs]
    r16 = lax.broadcasted_iota(jnp.int32, (PEER_TOPK, t), 0)

    def body(k, carry):
        out = []
        for (x, vals, idxs), rid in zip(carry, rids):
            m = jnp.max(x, axis=0, keepdims=True)
            i = jnp.min(jnp.where(x == m, rid, x.shape[0]), axis=0, keepdims=True)
            out.append((jnp.where(rid == i, NEG_INF, x), jnp.where(r16 == k, m, vals),
                        jnp.where(r16 == k, i, idxs)))
        return tuple(out)

    init = tuple((x, jnp.zeros((PEER_TOPK, t), F32), jnp.zeros((PEER_TOPK, t), jnp.int32)) for x in xs)
    res = lax.fori_loop(0, PEER_TOPK, body, init)
    return [(v, i) for _, v, i in res]


_PAIR_COUNT = [PEER_TOPK // (a + 1) for a in range(PEER_TOPK)]
_PAIR_START = [sum(_PAIR_COUNT[:a]) for a in range(PEER_TOPK)]
_N_PAIRS = sum(_PAIR_COUNT)
_PAIR_ROWS = -(-_N_PAIRS // SUBLANES) * SUBLANES


def _joint_top(v1, i1, v2, i2):
    t = v1.shape[1]
    rid = lax.broadcasted_iota(jnp.int32, (_PAIR_ROWS, t), 0)
    c1 = jnp.full((_PAIR_ROWS, t), NEG_INF, F32)
    k1 = jnp.zeros((_PAIR_ROWS, t), jnp.int32)
    brow = jnp.full((_PAIR_ROWS, t), -1, jnp.int32)
    for a in range(PEER_TOPK):
        in_a = jnp.logical_and(rid >= _PAIR_START[a], rid < _PAIR_START[a] + _PAIR_COUNT[a])
        c1 = jnp.where(in_a, v1[a:a + 1, :], c1)
        k1 = jnp.where(in_a, i1[a:a + 1, :], k1)
        brow = jnp.where(in_a, rid - _PAIR_START[a], brow)
    c2 = jnp.zeros((_PAIR_ROWS, t), F32)
    k2 = jnp.zeros((_PAIR_ROWS, t), jnp.int32)
    for b in range(PEER_TOPK):
        c2 = jnp.where(brow == b, v2[b:b + 1, :], c2)
        k2 = jnp.where(brow == b, i2[b:b + 1, :], k2)
    return c1 + c2, k1, k2


def _joint_finish(best, pos, k1, k2):
    e1 = jnp.zeros_like(pos)
    e2 = jnp.zeros_like(pos)
    for r in range(_N_PAIRS):
        e1 = jnp.where(pos == r, k1[r:r + 1, :], e1)
        e2 = jnp.where(pos == r, k2[r:r + 1, :], e2)
    return best, e1, e2


def _route_kernel(x_ref, modp_ref, mods_ref, nw_ref, wq_ref, sk_ref, h_ref, eid_ref, g_ref, hq_sc, *,
                  tiles_per_seq, n_ptiles, n_batch):
    i = pl.program_id(0)
    x = x_ref[...]
    n = x * lax.rsqrt(jnp.mean(x * x, axis=-1, keepdims=True) + EPS) * nw_ref[...]
    sh = _tile_mod(modp_ref, mods_ref, 0, i, tiles_per_seq, n_ptiles, n_batch, TM)
    sc = _tile_mod(modp_ref, mods_ref, 1, i, tiles_per_seq, n_ptiles, n_batch, TM)
    hb = (n * (1.0 + sc) + sh).astype(BF16)
    h_ref[...] = hb
    hq_sc[...] = jnp.dot(hb, wq_ref[...], preferred_element_type=F32)
    half = PEER_QDIM // 2
    for part in range(TM // LANES):
        rows = slice(part * LANES, (part + 1) * LANES)
        q1 = jnp.concatenate([hq_sc[rows, hd * PEER_QDIM:hd * PEER_QDIM + half] for hd in range(PEER_HEADS)], 0)
        q2 = jnp.concatenate([hq_sc[rows, hd * PEER_QDIM + half:(hd + 1) * PEER_QDIM]
                              for hd in range(PEER_HEADS)], 0)
        s1_all = lax.dot_general(sk_ref[0], q1, _NT, precision=HIGHEST, preferred_element_type=F32)
        s2_all = lax.dot_general(sk_ref[1], q2, _NT, precision=HIGHEST, preferred_element_type=F32)
        eids, gs = [], []
        pending = None
        for hd in range(PEER_HEADS + 1):
            cols = slice(hd * LANES, (hd + 1) * LANES)
            xs = [s1_all[:, cols], s2_all[:, cols]] if hd < PEER_HEADS else []
            res = _top16_rows(xs + ([pending[0]] if pending is not None else []))
            if pending is not None:
                best, pos = res[-1]
                best, e1, e2 = _joint_finish(best, pos, pending[1], pending[2])
                ex = jnp.exp(best - jnp.max(best, axis=0, keepdims=True))
                eids.append(e1 * PEER_NKEYS + e2)
                gs.append(ex / jnp.sum(ex, axis=0, keepdims=True))
            pending = _joint_top(res[0][0], res[0][1], res[1][0], res[1][1]) if hd < PEER_HEADS else None
        eid_ref[rows, :] = jnp.concatenate(eids, axis=0).T
        g_ref[rows, :] = jnp.concatenate(gs, axis=0).T


def _route(x, modp, mods, nw, wq, sub_keys, geom):
    n_pad, d = x.shape
    return pl.pallas_call(
        functools.partial(_route_kernel, **geom),
        grid=(n_pad // TM,),
        in_specs=[
            pl.BlockSpec((TM, d), lambda i: (i, 0)),
            pl.BlockSpec(modp.shape, lambda i: (0, 0, 0)),
            pl.BlockSpec(mods.shape, lambda i: (0, 0, 0)),
            pl.BlockSpec((1, d), lambda i: (0, 0)),
            pl.BlockSpec(wq.shape, lambda i: (0, 0)),
            pl.BlockSpec(sub_keys.shape, lambda i: (0, 0, 0)),
        ],
        out_specs=[
            pl.BlockSpec((TM, d), lambda i: (i, 0)),
            pl.BlockSpec((TM, PEER_SLOTS), lambda i: (i, 0)),
            pl.BlockSpec((TM, PEER_SLOTS), lambda i: (i, 0)),
        ],
        out_shape=[
            jax.ShapeDtypeStruct((n_pad, d), BF16),
            jax.ShapeDtypeStruct((n_pad, PEER_SLOTS), jnp.int32),
            jax.ShapeDtypeStruct((n_pad, PEER_SLOTS), F32),
        ],
        scratch_shapes=[pltpu.VMEM((TM, wq.shape[1]), F32)],
        compiler_params=_params("parallel"),
        name="peer_route",
    )(x, modp, mods, nw, wq, sub_keys)


def _gelu(a):
    return 0.5 * a * (1.0 + lax.erf(a * (1.0 / math.sqrt(2.0))))


def _expert_act_kernel(h_ref, u0_ref, unext_ref, eid_ref, g_ref, z_ref, a_sc):
    j = pl.program_id(1)
    rows_per_tile = TE // PEER_NKEYS
    cur = j % 2

    @pl.when(j == 0)
    def _():
        z_ref[...] = jnp.zeros_like(z_ref)
        a_sc[0] = lax.dot_general(h_ref[...], u0_ref[...], _NT, preferred_element_type=F32)

    a_sc[1 - cur] = lax.dot_general(h_ref[...], unext_ref[...], _NT, preferred_element_type=F32)

    for sb in range(TT // TA_ROWS):
        rows = slice(sb * TA_ROWS, (sb + 1) * TA_ROWS)
        eid = eid_ref[rows, :]
        k1 = eid // PEER_NKEYS
        k2 = eid % PEER_NKEYS
        act = z_ref[rows, :]
        for rl in range(rows_per_tile):
            picked = jnp.take_along_axis(a_sc[cur, rows, rl * PEER_NKEYS:(rl + 1) * PEER_NKEYS], k2, axis=1)
            act = jnp.where(k1 == j * rows_per_tile + rl, picked, act)
        z_ref[rows, :] = act

    @pl.when(j == pl.num_programs(1) - 1)
    def _():
        z_ref[...] = g_ref[...] * _gelu(z_ref[...])


def _expert_act(h2, u_bf, eid, g):
    n_pad, d = h2.shape
    n_exp = u_bf.shape[0]
    slot_block = pl.BlockSpec((TT, PEER_SLOTS), lambda i, j: (i, 0))
    n_te = n_exp // TE
    return pl.pallas_call(
        _expert_act_kernel,
        grid=(n_pad // TT, n_te),
        in_specs=[
            pl.BlockSpec((TT, d), lambda i, j: (i, 0)),
            pl.BlockSpec((TE, d), lambda i, j: (0, 0)),
            pl.BlockSpec((TE, d), lambda i, j: (jnp.minimum(j + 1, n_te - 1), 0)),
            slot_block, slot_block,
        ],
        out_specs=slot_block,
        out_shape=jax.ShapeDtypeStruct((n_pad, PEER_SLOTS), F32),
        scratch_shapes=[pltpu.VMEM((2, TT, TE), F32)],
        compiler_params=_params("parallel", "arbitrary"),
        name="peer_act",
    )(h2, u_bf, u_bf, eid, g)


def _expert_out_kernel(eid_ref, z_ref, v_ref, o_ref, zall_sc):
    j = pl.program_id(1)
    rows_per_tile = TE // PEER_NKEYS
    n_groups = PEER_NKEYS // rows_per_tile

    @pl.when(j == 0)
    def _():
        sub = lax.broadcasted_iota(jnp.int32, (PEER_NKEYS, PEER_SLOTS), 0)

        def token(t, carry):
            eid = eid_ref[pl.ds(t, 1), :]
            zrow = z_ref[pl.ds(t, 1), :]
            lhs = jnp.where(eid // PEER_NKEYS == sub, zrow, 0.0).astype(BF16)
            rhs = jnp.where(eid % PEER_NKEYS == sub, 1.0, 0.0).astype(BF16)
            zt = lax.dot_general(lhs, rhs, _NT, preferred_element_type=F32)
            base = pl.multiple_of(t * rows_per_tile, rows_per_tile)
            for gi in range(n_groups):
                zall_sc[gi, pl.ds(base, rows_per_tile), :] = zt[gi * rows_per_tile:(gi + 1) * rows_per_tile, :]
            return carry

        lax.fori_loop(0, TZ, token, 0, unroll=4)

    y = None
    for pair in range(rows_per_tile // 2):
        zr = jnp.concatenate(
            [zall_sc[j, pl.ds(2 * pair + r, TZ, stride=rows_per_tile), :] for r in range(2)], axis=1)
        part = jnp.dot(zr.astype(BF16), v_ref[2 * pair * PEER_NKEYS:(2 * pair + 2) * PEER_NKEYS, :],
                       preferred_element_type=F32)
        y = part if y is None else y + part

    @pl.when(j == 0)
    def _():
        o_ref[...] = y

    @pl.when(j != 0)
    def _():
        o_ref[...] = o_ref[...] + y


def _expert_out(eid, z, v_bf):
    n_pad = eid.shape[0]
    n_exp, d = v_bf.shape
    rows_per_tile = TE // PEER_NKEYS
    assert rows_per_tile == SUBLANES
    slot_block = pl.BlockSpec((TZ, PEER_SLOTS), lambda i, j: (i, 0))
    return pl.pallas_call(
        _expert_out_kernel,
        grid=(n_pad // TZ, n_exp // TE),
        in_specs=[slot_block, slot_block, pl.BlockSpec((TE, d), lambda i, j: (j, 0))],
        out_specs=pl.BlockSpec((TZ, d), lambda i, j: (i, 0)),
        out_shape=jax.ShapeDtypeStruct((n_pad, d), F32),
        scratch_shapes=[pltpu.VMEM((PEER_NKEYS // rows_per_tile, TZ * rows_per_tile, PEER_NKEYS), F32)],
        compiler_params=_params("parallel", "arbitrary"),
        name="peer_out",
    )(eid, z, v_bf)


def _residual_kernel(x_ref, y_ref, modp_ref, mods_ref, o_ref, *, tiles_per_seq, n_ptiles, n_batch):
    i = pl.program_id(0)
    g2 = _tile_mod(modp_ref, mods_ref, 0, i, tiles_per_seq, n_ptiles, n_batch, TM)
    o_ref[...] = x_ref[...] + g2 * y_ref[...]


def _residual(x, y, modp, mods, geom):
    n_pad, d = x.shape
    return pl.pallas_call(
        functools.partial(_residual_kernel, **geom),
        grid=(n_pad // TM,),
        in_specs=[
            pl.BlockSpec((TM, d), lambda i: (i, 0)),
            pl.BlockSpec((TM, d), lambda i: (i, 0)),
            pl.BlockSpec(modp.shape, lambda i: (0, 0, 0)),
            pl.BlockSpec(mods.shape, lambda i: (0, 0, 0)),
        ],
        out_specs=pl.BlockSpec((TM, d), lambda i: (i, 0)),
        out_shape=jax.ShapeDtypeStruct((n_pad, d), F32),
        compiler_params=_params("parallel"),
        name="gated_residual",
    )(x, y, modp, mods)


def _rel_bucket(dist):
    n = jnp.maximum(dist, 0)
    max_exact = N_BUCKETS // 2
    nf = jnp.maximum(n, 1).astype(F32)
    large = max_exact + (jnp.log(nf / max_exact) / math.log(MAX_DISTANCE / max_exact)
                         * (N_BUCKETS - max_exact)).astype(jnp.int32)
    return jnp.where(n < max_exact, n, jnp.minimum(large, N_BUCKETS - 1))


def _bias_tiles(rel_bias):
    assert MOBA_BLOCK + 1 >= MAX_DISTANCE
    blk = MOBA_BLOCK

    def toeplitz(dist):
        u = rel_bias[_rel_bucket(dist)].T
        flat = jnp.tile(u, (1, blk + 1))[:, :blk * (2 * blk + 1)]
        return flat.reshape(N_HEADS, blk, 2 * blk + 1)[:, :, :blk][:, :, ::-1]

    k = jnp.arange(2 * blk)
    own = toeplitz(jnp.maximum(k - (blk - 1), 0))
    prev = toeplitz(jnp.minimum(k + 1, 2 * blk - 1))
    tiles = jnp.stack([own, prev], axis=1)
    far = jnp.broadcast_to(rel_bias[N_BUCKETS - 1][:, None, None], (N_HEADS, SUBLANES, LANES))
    return tiles, far


def kernel(x_prompt, x_sample, c_prompt, c_sample, cache_k, cache_v, state_pool, page_table, rel_bias,
           w_mod, b_mod, norm1_w, w_in, q_norm_w, k_norm_w, pool_w, pool_scale, w_pa, w_pb, w_out,
           norm2_w, peer_wq, peer_sub_keys, peer_u, peer_v):
    n_batch, seq, d = x_prompt.shape
    db, t_new, _ = x_sample.shape
    depth = w_mod.shape[0]
    n_pages = page_table.shape[1]
    past_len = n_pages * PAGE_SIZE
    n_prompt = n_batch * seq
    n_samp = db * t_new
    n_past_blocks = past_len // MOBA_BLOCK
    assert past_len % MOBA_BLOCK == 0 and n_past_blocks >= MOBA_TOPK
    assert seq % TP == 0 and seq % MOBA_BLOCK == 0 and n_samp <= TM and n_samp % SUBLANES == 0
    assert PEER_SLOTS == PEER_NKEYS and MOBA_BLOCK == QB and TM % LANES == 0
    assert t_new <= SUBLANES and n_batch <= SUBLANES and n_batch + db <= 16
    assert n_prompt % TM == 0
    step = math.lcm(TM, TT, TZ)
    n_pad = -(-(n_prompt + n_samp) // step) * step
    geom = dict(tiles_per_seq=seq // TM, n_ptiles=n_prompt // TM, n_batch=n_batch)

    off_ga, off_gb, off_u = 0, d, 2 * d
    off_q = off_u + POOL_WIDTH
    off_k = off_q + ATT_WIDTH
    off_v = off_k + ATT_WIDTH
    o_q = POOL_WIDTH
    o_ga = o_q + 3 * ATT_WIDTH
    w_in_bf = jnp.concatenate([w_in[:, :, o_ga:], w_in[:, :, :o_ga]], axis=-1).astype(BF16)
    w_pa_bf, w_pb_bf, w_out_bf = w_pa.astype(BF16), w_pb.astype(BF16), w_out.astype(BF16)
    wq_bf, u_bf, v_bf = peer_wq.astype(BF16), peer_u.astype(BF16), peer_v.astype(BF16)

    c16 = jnp.zeros((16, d), F32).at[:n_batch].set(c_prompt).at[n_batch:n_batch + db].set(c_sample)
    mod = _modulation(c16, w_mod, b_mod).reshape(depth, 16, 6, d).transpose(0, 2, 1, 3)
    modp_all = mod[:, :, :SUBLANES]
    mods_all = jnp.repeat(mod[:, :, n_batch:n_batch + db], t_new, axis=2)

    bias_tiles, far = _bias_tiles(rel_bias)
    cache_k4, cache_v4 = _head_major(cache_k), _head_major(cache_v)
    kmean = _cache_block_means(cache_k4, page_table).reshape(db, n_past_blocks, N_HEADS, depth, HEAD_DIM)

    x = jnp.zeros((n_pad, d), F32).at[:n_prompt].set(x_prompt.reshape(n_prompt, d))
    x = x.at[n_prompt:n_prompt + n_samp].set(x_sample.reshape(n_samp, d))

    k_rows, v_rows, pool_p, pool_s = [], [], [], []
    for l in range(depth):
        z = _mixer_in(x, modp_all[l, 0:2], mods_all[l, 0:2], norm1_w[l][None], w_in_bf[l],
                      q_norm_w[l][None], k_norm_w[l][None], geom)
        k_rows.append(z[:, off_k:off_k + ATT_WIDTH])
        v_rows.append(z[:, off_v:off_v + ATT_WIDTH])
        u_p = z[:n_prompt, off_u:off_u + POOL_WIDTH].reshape(n_batch, seq, POOL_WIDTH)
        pool_p.append(u_p[:, seq - POOL_STATE:])
        zs = z[n_prompt:n_prompt + n_samp].reshape(db, t_new, -1)
        u_s = zs[:, :, off_u:off_u + POOL_WIDTH]
        pool_s.append(jnp.concatenate([state_pool[l], u_s], axis=1)[:, -POOL_STATE:])

        ypool_p = _pool_prompt(z, pool_w[l], pool_scale[l][None], n_batch, seq, off_u)
        yatt_p = _moba_prompt(z, bias_tiles, far, n_batch, seq, off_q)

        zs8 = jnp.pad(zs, ((0, 0), (0, SUBLANES - t_new), (0, 0)))
        state16 = jnp.pad(state_pool[l], ((0, 0), (POOL_HALO - POOL_STATE, 0), (0, 0)))
        kmean_l = kmean[:, :, :, l].transpose(0, 2, 1, 3)
        ypool_s, idx = _sample_pre(zs8, state16, kmean_l, pool_w[l], pool_scale[l][None], past_len,
                                   off_u, off_q)
        sel = idx[:, :, :t_new, :MOBA_TOPK]
        ppb = MOBA_BLOCK // PAGE_SIZE
        logical = sel[..., None] * ppb + jnp.arange(ppb)
        pages = page_table[jnp.arange(db)[:, None, None, None, None], logical]
        yatt_s = _sample_att(pages.reshape(-1), sel.reshape(-1), zs8, bias_tiles, far, cache_k4, cache_v4,
                             l, depth, t_new, n_past_blocks, off_q)

        tail = n_pad - n_prompt - n_samp
        ypool = jnp.concatenate([ypool_p, ypool_s[:, :t_new].reshape(n_samp, POOL_WIDTH),
                                 jnp.zeros((tail, POOL_WIDTH), F32)], axis=0)
        yatt = jnp.concatenate([yatt_p, yatt_s[:, :t_new].reshape(n_samp, ATT_WIDTH),
                                jnp.zeros((tail, ATT_WIDTH), F32)], axis=0)
        x1 = _merge(ypool, yatt, z, x, modp_all[l, 2:3], mods_all[l, 2:3], w_pa_bf[l], w_pb_bf[l],
                    w_out_bf[l], geom)
        h2, eid, g = _route(x1, modp_all[l, 3:5], mods_all[l, 3:5], norm2_w[l][None], wq_bf[l],
                            peer_sub_keys[l], geom)
        y = _expert_out(eid, _expert_act(h2, u_bf[l], eid, g), v_bf[l])
        x = _residual(x1, y, modp_all[l, 5:6], mods_all[l, 5:6], geom)

    def rows(parts, lo, hi, lead):
        return jnp.stack([p[lo:hi] for p in parts], axis=1).reshape(*lead, depth, N_HEADS, HEAD_DIM)

    y_prompt = x[:n_prompt].reshape(n_batch, seq, d)
    y_sample = x[n_prompt:n_prompt + n_samp].reshape(db, t_new, d)
    return (y_prompt, y_sample,
            rows(k_rows, 0, n_prompt, (n_batch, seq)), rows(v_rows, 0, n_prompt, (n_batch, seq)),
            jnp.stack(pool_p, axis=0),
            rows(k_rows, n_prompt, n_prompt + n_samp, (db, t_new)),
            rows(v_rows, n_prompt, n_prompt + n_samp, (db, t_new)),
            jnp.stack(pool_s, axis=0))
```
